```python
import math
import jax, jax.numpy as jnp
from jax import lax
import numpy as np

D_MODEL = 1024
BATCH = 2
SEQ = 8192
DEPTH = 4
DEC_BATCH = 32
DEC_SEQ = 4
PAST_LEN = 8192
PAGE_SIZE = 128

N_MIXERS = 2
N_ATT_LAYERS = (DEPTH + 1) // 2
N_RWKV_LAYERS = DEPTH // 2
ATT_HEADS = 8
ATT_HEAD_DIM = D_MODEL // (2 * ATT_HEADS)
ATT_V_DIM = 2 * ATT_HEAD_DIM
N_BUCKETS = 32
MAX_EXACT = N_BUCKETS // 2
MAX_DISTANCE = 128
Q_BLOCK = 128
RW_HEAD = 64
RW_HEADS = D_MODEL // RW_HEAD
D_DECAY_LORA = 64
D_AAA_LORA = 64
D_MV_LORA = 32
D_GATE_LORA = 160
D_FF = 4 * D_MODEL
PLE_DIM = 256
NORM_EPS = 1e-6
SUBLN_EPS = 1e-5
GN_EPS = 64e-5
NEG_INF = -1e30
F32 = jnp.float32

kernel_name = 'diffattn_rwkv7_hybrid_step'


def _rmsnorm(x, g, eps=NORM_EPS):
    xf = x.astype(F32)
    y = xf * lax.rsqrt(jnp.mean(xf * xf, axis=-1, keepdims=True) + eps)
    return (y * g.astype(F32)).astype(x.dtype)


def _rel_bucket(qpos, kpos):
    n = jnp.maximum(qpos[:, None] - kpos[None, :], 0)
    nf = jnp.maximum(n, 1).astype(F32)
    large = MAX_EXACT + (jnp.log(nf / MAX_EXACT) / math.log(MAX_DISTANCE / MAX_EXACT)
                         * (N_BUCKETS - MAX_EXACT)).astype(jnp.int32)
    large = jnp.minimum(large, N_BUCKETS - 1)
    return jnp.where(n < MAX_EXACT, n, large)


def _diff_lambda(lq1, lk1, lq2, lk2, lam_init):
    e1 = jnp.exp(jnp.sum(lq1.astype(F32) * lk1.astype(F32)))
    e2 = jnp.exp(jnp.sum(lq2.astype(F32) * lk2.astype(F32)))
    return e1 - e2 + lam_init


def _diff_qkv(xn, w_qkv):
    b, t, _ = xn.shape
    q, k, v = jnp.split(xn @ w_qkv, 3, axis=-1)
    q = q.reshape(b, t, ATT_HEADS, 2, ATT_HEAD_DIM)
    k = k.reshape(b, t, ATT_HEADS, 2, ATT_HEAD_DIM)
    v = v.reshape(b, t, ATT_HEADS, ATT_V_DIM)
    return q, k, v


def _diff_core(q, k, v, qpos, kpos, rel_bias, lam):
    logits = jnp.einsum('bqhcd,bkhcd->bhcqk', q.astype(F32), k.astype(F32)) * (ATT_HEAD_DIM ** -0.5)
    bias = jnp.transpose(rel_bias.astype(F32)[_rel_bucket(qpos, kpos)], (2, 0, 1))
    logits = logits + bias[None, :, None]
    valid = kpos[None, :] <= qpos[:, None]
    logits = jnp.where(valid, logits, NEG_INF)
    probs = jax.nn.softmax(logits, axis=-1)
    attn = probs[:, :, 0] - lam * probs[:, :, 1]
    return jnp.einsum('bhqk,bkhe->bqhe', attn, v.astype(F32))


def _diff_out(o, subln_g, lam_init, w_o, dtype):
    b, t = o.shape[0], o.shape[1]
    o = _rmsnorm(o, subln_g, SUBLN_EPS) * (1.0 - lam_init)
    return o.reshape(b, t, D_MODEL).astype(dtype) @ w_o


def _diff_attn_prompt(xn, w_qkv, w_o, lam, lam_init, subln_g, rel_bias):
    b, t, _ = xn.shape
    q, k, v = _diff_qkv(xn, w_qkv)
    nb = t // Q_BLOCK
    q_blocks = jnp.moveaxis(q.reshape(b, nb, Q_BLOCK, ATT_HEADS, 2, ATT_HEAD_DIM), 1, 0)
    starts = jnp.arange(nb, dtype=jnp.int32) * Q_BLOCK
    kpos = jnp.arange(t, dtype=jnp.int32)

    def block(args):
        qb, q0 = args
        qpos = q0 + jnp.arange(Q_BLOCK, dtype=jnp.int32)
        return _diff_core(qb, k, v, qpos, kpos, rel_bias, lam)

    o = lax.map(block, (q_blocks, starts))
    o = jnp.moveaxis(o, 0, 1).reshape(b, t, ATT_HEADS, ATT_V_DIM)
    y = _diff_out(o, subln_g, lam_init, w_o, xn.dtype)
    return y, k.reshape(b, t, ATT_HEADS, 2 * ATT_HEAD_DIM), v


def _diff_attn_sample(xn, cache_k, cache_v, layer, page_table, w_qkv, w_o, lam, lam_init, subln_g, rel_bias):
    b, t, _ = xn.shape
    q, k, v = _diff_qkv(xn, w_qkv)
    past = page_table.shape[1] * PAGE_SIZE
    kp = cache_k[layer, page_table].reshape(b, past, ATT_HEADS, 2, ATT_HEAD_DIM).astype(k.dtype)
    vp = cache_v[layer, page_table].reshape(b, past, ATT_HEADS, ATT_V_DIM).astype(v.dtype)
    k_all = jnp.concatenate([kp, k], axis=1)
    v_all = jnp.concatenate([vp, v], axis=1)
    qpos = past + jnp.arange(t, dtype=jnp.int32)
    kpos = jnp.arange(past + t, dtype=jnp.int32)
    o = _diff_core(q, k_all, v_all, qpos, kpos, rel_bias, lam)
    y = _diff_out(o, subln_g, lam_init, w_o, xn.dtype)
    return y, k.reshape(b, t, ATT_HEADS, 2 * ATT_HEAD_DIM), v


def _wkv_step(s, inp):
    r_t, w_t, k_t, v_t, a_t, b_t = inp
    sa = jnp.einsum('bhij,bhj->bhi', s, a_t)
    s = s * w_t[:, :, None, :] + sa[..., None] * b_t[:, :, None, :] + v_t[..., None] * k_t[:, :, None, :]
    return s, jnp.einsum('bhij,bhj->bhi', s, r_t)


def _rwkv_time_mix(xn, shift_row, s0, v_first, mu, w_rkv, w0, w1, w2, a0, a1, a2,
                   g1, g2, k_k, k_a, r_k, ln_w, ln_b, w_o, vres):
    b, t, d = xn.shape
    x_prev = jnp.concatenate([shift_row[:, None].astype(xn.dtype), xn[:, :-1]], axis=1)
    xx = x_prev - xn
    xs = xn[:, :, None] + xx[:, :, None] * mu
    xr, xw, xk, xv, xa, xg = [xs[:, :, i] for i in range(6)]
    rkv = jnp.einsum('btnd,nde->btne', jnp.stack([xr, xk, xv], axis=2), w_rkv)
    r, k, v = rkv[:, :, 0], rkv[:, :, 1], rkv[:, :, 2]
    if vres is None:
        v_first = v
    else:
        v0, v1, v2 = vres
        v = v + (v_first - v) * jax.nn.sigmoid(v0 + (xv @ v1) @ v2)
    w_log = -jax.nn.softplus(-(w0 + jnp.tanh(xw @ w1) @ w2).astype(F32)) - 0.5
    decay = jnp.exp(-jnp.exp(w_log))
    a = jax.nn.sigmoid((a0 + (xa @ a1) @ a2).astype(F32))
    g = jax.nn.sigmoid(xg @ g1) @ g2

    def heads(z):
        return z.astype(F32).reshape(b, t, RW_HEADS, RW_HEAD)

    kk = heads(k * k_k)
    kk = kk / jnp.maximum(jnp.sqrt(jnp.sum(kk * kk, axis=-1, keepdims=True)), 1e-12)
    k_mod = heads(k.astype(F32) * (1.0 + (a - 1.0) * k_a.astype(F32)))
    r_h, v_h, a_h, w_h = heads(r), heads(v), heads(a), heads(decay)
    seq_in = tuple(jnp.moveaxis(z, 1, 0) for z in (r_h, w_h, k_mod, v_h, -kk, kk * a_h))
    s_final, ys = lax.scan(_wkv_step, s0.astype(F32), seq_in)
    y = jnp.moveaxis(ys, 0, 1)
    mean = jnp.mean(y, axis=-1, keepdims=True)
    var = jnp.mean(jnp.square(y - mean), axis=-1, keepdims=True)
    y = ((y - mean) * lax.rsqrt(var + GN_EPS) * ln_w.astype(F32).reshape(RW_HEADS, RW_HEAD)
         + ln_b.astype(F32).reshape(RW_HEADS, RW_HEAD))
    bonus = jnp.sum(r_h * k_mod * r_k.astype(F32), axis=-1, keepdims=True) * v_h
    out = ((y + bonus).reshape(b, t, d).astype(xn.dtype) * g) @ w_o
    return out, s_final.astype(s0.dtype), xn[:, -1], v_first


def _ffn_ple(x, p, g_mlp, w1, w2, g_ple, w_in, w_gate):
    h = _rmsnorm(x, g_mlp)
    x = x + jnp.square(jax.nn.relu(h @ w1)) @ w2
    gate = jax.nn.sigmoid(_rmsnorm(x, g_ple) @ w_gate)
    return x + (p.astype(x.dtype) @ w_in) * gate


def setup_inputs(seed: int = 0) -> dict:
    key = jax.random.key(seed)
    keys = iter(jax.random.split(key, 64))

    def nrm(shape, scale):
        return jax.random.normal(next(keys), shape, F32) * scale

    def gain(shape):
        return 1.0 + nrm(shape, 0.02)

    D, NA, NR = D_MODEL, N_ATT_LAYERS, N_RWKV_LAYERS
    n_pages = PAST_LEN // PAGE_SIZE
    n_used = DEC_BATCH * n_pages
    n_pool = n_used + n_used // 4
    inp = {}
    inp['x_prompt'] = nrm((BATCH, SEQ, D), 1.0)
    inp['x_sample'] = nrm((DEC_BATCH, DEC_SEQ, D), 1.0)
    inp['p_prompt'] = nrm((DEPTH, BATCH, SEQ, PLE_DIM), 1.0)
    inp['p_sample'] = nrm((DEPTH, DEC_BATCH, DEC_SEQ, PLE_DIM), 1.0)
    inp['cache_k'] = nrm((NA, n_pool, PAGE_SIZE, ATT_HEADS, 2 * ATT_HEAD_DIM), 1.0)
    inp['cache_v'] = nrm((NA, n_pool, PAGE_SIZE, ATT_HEADS, ATT_V_DIM), 1.0)
    inp['state_wkv'] = nrm((NR, DEC_BATCH, RW_HEADS, RW_HEAD, RW_HEAD), 0.5)
    inp['state_shift'] = nrm((NR, DEC_BATCH, D), 1.0)
    perm = jax.random.permutation(next(keys), n_pool)
    inp['page_table'] = perm[:n_used].reshape(DEC_BATCH, n_pages).astype(jnp.int32)
    inp['rel_bias'] = nrm((N_BUCKETS, ATT_HEADS), 0.5)
    inp['norm_mix'] = gain((DEPTH, D))
    inp['norm_mlp'] = gain((DEPTH, D))
    inp['norm_ple'] = gain((DEPTH, D))
    inp['norm_final'] = gain((D,))
    inp['att_w_qkv'] = nrm((NA, D, 3 * D), D ** -0.5)
    inp['att_w_o'] = nrm((NA, D, D), D ** -0.5)
    inp['att_lam_q1'] = nrm((NA, ATT_HEAD_DIM), 0.1)
    inp['att_lam_k1'] = nrm((NA, ATT_HEAD_DIM), 0.1)
    inp['att_lam_q2'] = nrm((NA, ATT_HEAD_DIM), 0.1)
    inp['att_lam_k2'] = nrm((NA, ATT_HEAD_DIM), 0.1)
    inp['att_subln'] = gain((NA, ATT_V_DIM))
    inp['rw_mu'] = jax.random.uniform(next(keys), (NR, 6, D), F32, 0.0, 1.0)
    inp['rw_w_rkv'] = nrm((NR, 3, D, D), D ** -0.5)
    inp['rw_w0'] = jax.random.uniform(next(keys), (NR, D), F32, -6.0, -1.0)
    inp['rw_w1'] = nrm((NR, D, D_DECAY_LORA), D ** -0.5)
    inp['rw_w2'] = nrm((NR, D_DECAY_LORA, D), 0.1 * D_DECAY_LORA ** -0.5)
    inp['rw_a0'] = nrm((NR, D), 0.1)
    inp['rw_a1'] = nrm((NR, D, D_AAA_LORA), D ** -0.5)
    inp['rw_a2'] = nrm((NR, D_AAA_LORA, D), 0.1 * D_AAA_LORA ** -0.5)
    inp['rw_v0'] = 0.5 + nrm((max(NR - 1, 0), D), 0.1)
    inp['rw_v1'] = nrm((max(NR - 1, 0), D, D_MV_LORA), D ** -0.5)
    inp['rw_v2'] = nrm((max(NR - 1, 0), D_MV_LORA, D), 0.1 * D_MV_LORA ** -0.5)
    inp['rw_g1'] = nrm((NR, D, D_GATE_LORA), D ** -0.5)
    inp['rw_g2'] = nrm((NR, D_GATE_LORA, D), D_GATE_LORA ** -0.5)
    inp['rw_k_k'] = 0.85 + nrm((NR, D), 0.02)
    inp['rw_k_a'] = 1.0 + nrm((NR, D), 0.02)
    inp['rw_r_k'] = nrm((NR, RW_HEADS, RW_HEAD), 0.1)
    inp['rw_ln_w'] = gain((NR, D))
    inp['rw_ln_b'] = nrm((NR, D), 0.02)
    inp['rw_w_o'] = nrm((NR, D, D), D ** -0.5)
    inp['mlp_w1'] = nrm((DEPTH, D, D_FF), D ** -0.5)
    inp['mlp_w2'] = nrm((DEPTH, D_FF, D), D_FF ** -0.5)
    inp['ple_w_in'] = nrm((DEPTH, PLE_DIM, D), PLE_DIM ** -0.5)
    inp['ple_w_gate'] = nrm((DEPTH, D, D), D ** -0.5)
    return inp


def reference(x_prompt, x_sample, p_prompt, p_sample, cache_k, cache_v, state_wkv, state_shift,
              page_table, rel_bias, norm_mix, norm_mlp, norm_ple, norm_final,
              att_w_qkv, att_w_o, att_lam_q1, att_lam_k1, att_lam_q2, att_lam_k2, att_subln,
              rw_mu, rw_w_rkv, rw_w0, rw_w1, rw_w2, rw_a0, rw_a1, rw_a2, rw_v0, rw_v1, rw_v2,
              rw_g1, rw_g2, rw_k_k, rw_k_a, rw_r_k, rw_ln_w, rw_ln_b, rw_w_o,
              mlp_w1, mlp_w2, ple_w_in, ple_w_gate):
    xp, xs = x_prompt, x_sample
    bp, bs = x_prompt.shape[0], x_sample.shape[0]
    vf_p, vf_s = None, None
    kp_l, vp_l, ks_l, vs_l = [], [], [], []
    wkvp_l, shp_l, wkvs_l, shs_l = [], [], [], []
    for i in range(DEPTH):
        hp = _rmsnorm(xp, norm_mix[i])
        hs = _rmsnorm(xs, norm_mix[i])
        if i % N_MIXERS == 0:
            a = i // N_MIXERS
            lam_init = 0.8 - 0.6 * math.exp(-0.3 * i)
            lam = _diff_lambda(att_lam_q1[a], att_lam_k1[a], att_lam_q2[a], att_lam_k2[a], lam_init)
            mp, k_new_p, v_new_p = _diff_attn_prompt(hp, att_w_qkv[a], att_w_o[a], lam, lam_init,
                                                     att_subln[a], rel_bias)
            ms, k_new_s, v_new_s = _diff_attn_sample(hs, cache_k, cache_v, a, page_table, att_w_qkv[a],
                                                     att_w_o[a], lam, lam_init, att_subln[a], rel_bias)
            kp_l.append(k_new_p)
            vp_l.append(v_new_p)
            ks_l.append(k_new_s)
            vs_l.append(v_new_s)
        else:
            r = i // N_MIXERS
            vres = None if r == 0 else (rw_v0[r - 1], rw_v1[r - 1], rw_v2[r - 1])
            prm = (rw_mu[r], rw_w_rkv[r], rw_w0[r], rw_w1[r], rw_w2[r], rw_a0[r], rw_a1[r], rw_a2[r],
                   rw_g1[r], rw_g2[r], rw_k_k[r], rw_k_a[r], rw_r_k[r], rw_ln_w[r], rw_ln_b[r], rw_w_o[r])
            shift0 = jnp.zeros((bp, D_MODEL), hp.dtype)
            s0 = jnp.zeros((bp, RW_HEADS, RW_HEAD, RW_HEAD), hp.dtype)
            mp, s_p, last_p, vf_p = _rwkv_time_mix(hp, shift0, s0, vf_p, *prm, vres)
            ms, s_s, last_s, vf_s = _rwkv_time_mix(hs, state_shift[r], state_wkv[r], vf_s, *prm, vres)
            wkvp_l.append(s_p)
            shp_l.append(last_p)
            wkvs_l.append(s_s)
            shs_l.append(last_s)
        xp = xp + mp
        xs = xs + ms
        xp = _ffn_ple(xp, p_prompt[i], norm_mlp[i], mlp_w1[i], mlp_w2[i], norm_ple[i], ple_w_in[i], ple_w_gate[i])
        xs = _ffn_ple(xs, p_sample[i], norm_mlp[i], mlp_w1[i], mlp_w2[i], norm_ple[i], ple_w_in[i], ple_w_gate[i])
    y_prompt = _rmsnorm(xp, norm_final)
    y_sample = _rmsnorm(xs, norm_final)
    k_prompt = jnp.stack(kp_l)
    v_prompt = jnp.stack(vp_l)
    wkv_prompt = jnp.stack(wkvp_l)
    shift_prompt = jnp.stack(shp_l)
    k_sample = jnp.stack(ks_l)
    v_sample = jnp.stack(vs_l)
    wkv_sample = jnp.stack(wkvs_l)
    shift_sample = jnp.stack(shs_l)
    return (y_prompt, y_sample, k_prompt, v_prompt, wkv_prompt, shift_prompt,
            k_sample, v_sample, wkv_sample, shift_sample)
```

```python
import functools
import math

import numpy as np
import jax
import jax.numpy as jnp
from jax import lax
from jax.experimental import pallas as pl
from jax.experimental.pallas import tpu as pltpu

F32 = jnp.float32
BF16 = jnp.bfloat16
HIGHEST = lax.Precision.HIGHEST

ATT_HEADS = 8
ATT_HEAD_DIM = 64
ATT_V_DIM = 128
N_BUCKETS = 32
MAX_EXACT = 16
MAX_DISTANCE = 128
PAGE_SIZE = 128
RW_HEAD = 64
NORM_EPS = 1e-6
SUBLN_EPS = 1e-5
GN_EPS = 64e-5
NEG_INF = -1e30

LANES = 128
SUBLANES = 8
VMEM_LIMIT = 56 * 1024 * 1024
TOKEN_TILE = 512
RWKV_TILE = 256
ATT_BLOCK = 512
WKV_CHUNK = 64
FF_CHUNK = 1024


def _params(n_axes):
    return pltpu.CompilerParams(dimension_semantics=("arbitrary",) * n_axes,
                                vmem_limit_bytes=VMEM_LIMIT)


def _rms(x, g, eps):
    return x * lax.rsqrt(jnp.mean(x * x, axis=-1, keepdims=True) + eps) * g


def _sigmoid(x):
    return 1.0 / (1.0 + jnp.exp(-x))


def _bdot(a, b):
    return jnp.dot(a, b, preferred_element_type=F32)


def _hdot(a, b):
    return jnp.dot(a, b, precision=HIGHEST, preferred_element_type=F32)


def _hdot_nt(a, b):
    return lax.dot_general(a, b, (((1,), (1,)), ((), ())), precision=HIGHEST,
                           preferred_element_type=F32)


def _hdot_tn(a, b):
    return lax.dot_general(a, b, (((0,), (0,)), ((), ())), precision=HIGHEST,
                           preferred_element_type=F32)


def _full(shape):
    zeros = (0,) * len(shape)
    return pl.BlockSpec(shape, lambda *_: zeros)


def _qkv_kernel(x_ref, g_ref, w_ref, qb_ref, k_ref, v_ref, kb_ref, vb_ref):
    d = x_ref.shape[1]
    xn = _rms(x_ref[...], g_ref[...], NORM_EPS).astype(BF16)
    q = _bdot(xn, w_ref[:, 0:d])
    qb_ref[...] = (q * (ATT_HEAD_DIM ** -0.5)).astype(BF16)
    k = _bdot(xn, w_ref[:, d:2 * d])
    k_ref[...] = k
    kb_ref[...] = k.astype(BF16)
    v = _bdot(xn, w_ref[:, 2 * d:3 * d])
    v_ref[...] = v
    vb_ref[...] = v.astype(BF16)


def _qkv_proj(x2, g, w_bf):
    m, d = x2.shape
    tm = min(TOKEN_TILE, m)
    row = pl.BlockSpec((tm, d), lambda i: (i, 0))
    return pl.pallas_call(
        _qkv_kernel,
        grid=(m // tm,),
        in_specs=[row, _full((1, d)), _full((d, 3 * d))],
        out_specs=[row, row, row, row, row],
        out_shape=[jax.ShapeDtypeStruct((m, d), BF16),
                   jax.ShapeDtypeStruct((m, d), F32),
                   jax.ShapeDtypeStruct((m, d), F32),
                   jax.ShapeDtypeStruct((m, d), BF16),
                   jax.ShapeDtypeStruct((m, d), BF16)],
        compiler_params=_params(1),
        name="qkv_proj",
    )(x2, g.reshape(1, d), w_bf)


def _proj_res_kernel(a_ref, res_ref, w_ref, o_ref):
    o_ref[...] = res_ref[...] + _bdot(a_ref[...].astype(BF16), w_ref[...])


def _proj_gate_res_kernel(a_ref, gate_ref, res_ref, w_ref, o_ref):
    a = (a_ref[...] * gate_ref[...]).astype(BF16)
    o_ref[...] = res_ref[...] + _bdot(a, w_ref[...])


def _proj_res(a2, res2, w_bf, gate2=None):
    m, d = res2.shape
    tm = min(TOKEN_TILE, m)
    row = pl.BlockSpec((tm, d), lambda i: (i, 0))
    if gate2 is None:
        kern, ins, specs = _proj_res_kernel, (a2, res2, w_bf), [row, row, _full((d, d))]
    else:
        kern, ins = _proj_gate_res_kernel, (a2, gate2, res2, w_bf)
        specs = [row, row, row, _full((d, d))]
    return pl.pallas_call(
        kern,
        grid=(m // tm,),
        in_specs=specs,
        out_specs=row,
        out_shape=jax.ShapeDtypeStruct((m, d), F32),
        compiler_params=_params(1),
        name="proj_res",
    )(*ins)


def _ffn_kernel(x_ref, p_ref, gm_ref, w1_ref, w2_ref, gp_ref, win_ref, wg_ref, gf_ref,
                o_ref, *y_ref):
    x = x_ref[...]
    h = _rms(x, gm_ref[...], NORM_EPS).astype(BF16)
    acc = x
    d_ff = w1_ref.shape[1]
    for c in range(d_ff // FF_CHUNK):
        cols = slice(c * FF_CHUNK, (c + 1) * FF_CHUNK)
        a = jnp.maximum(_bdot(h, w1_ref[:, cols]), 0.0)
        acc = acc + _bdot((a * a).astype(BF16), w2_ref[cols, :])
    gate = _sigmoid(_bdot(_rms(acc, gp_ref[...], NORM_EPS).astype(BF16), wg_ref[...]))
    out = acc + _bdot(p_ref[...].astype(BF16), win_ref[...]) * gate
    o_ref[...] = out
    if y_ref:
        y_ref[0][...] = _rms(out, gf_ref[...], NORM_EPS)


def _ffn_ple(x2, p2, g_mlp, w1_bf, w2_bf, g_ple, win_bf, wg_bf, g_final, emit_final):
    m, d = x2.shape
    pd = p2.shape[1]
    d_ff = w1_bf.shape[1]
    tm = min(TOKEN_TILE, m)
    row = pl.BlockSpec((tm, d), lambda i: (i, 0))
    once = dict(pipeline_mode=pl.Buffered(1))
    specs = [row, pl.BlockSpec((tm, pd), lambda i: (i, 0)), _full((1, d)),
             pl.BlockSpec((d, d_ff), lambda i: (0, 0), **once),
             pl.BlockSpec((d_ff, d), lambda i: (0, 0), **once),
             _full((1, d)),
             pl.BlockSpec((pd, d), lambda i: (0, 0), **once),
             pl.BlockSpec((d, d), lambda i: (0, 0), **once),
             _full((1, d))]
    n_out = 2 if emit_final else 1
    out = pl.pallas_call(
        _ffn_kernel,
        grid=(m // tm,),
        in_specs=specs,
        out_specs=[row] * n_out,
        out_shape=[jax.ShapeDtypeStruct((m, d), F32)] * n_out,
        compiler_params=_params(1),
        name="ffn_ple",
    )(x2, p2, g_mlp.reshape(1, d), w1_bf, w2_bf, g_ple.reshape(1, d), win_bf, wg_bf,
      g_final.reshape(1, d))
    return out if emit_final else (out[0], None)


def _bucket_np(n):
    n = np.asarray(n, np.int64)
    nf = np.maximum(n, 1).astype(np.float32)
    large = MAX_EXACT + (np.log(nf / np.float32(MAX_EXACT))
                         / np.float32(math.log(MAX_DISTANCE / MAX_EXACT))
                         * np.float32(N_BUCKETS - MAX_EXACT)).astype(np.int32)
    large = np.minimum(large, N_BUCKETS - 1)
    return np.where(n < MAX_EXACT, n, large).astype(np.int32)


def _bias_kernel(relb_ref, idx_ref, o_ref):
    h = pl.program_id(0)
    idx = idx_ref[...]
    acc = jnp.full(idx.shape, NEG_INF, F32)
    for b in range(N_BUCKETS):
        acc = jnp.where(idx == b, relb_ref[b, h], acc)
    o_ref[0] = acc


def _bias_tables(rel_bias, idx):
    r, c = idx.shape
    return pl.pallas_call(
        _bias_kernel,
        grid=(ATT_HEADS,),
        in_specs=[pl.BlockSpec(memory_space=pltpu.SMEM), _full((r, c))],
        out_specs=pl.BlockSpec((1, r, c), lambda h: (h, 0, 0)),
        out_shape=jax.ShapeDtypeStruct((ATT_HEADS, r, c), F32),
        compiler_params=_params(1),
        name="bias_tables",
    )(rel_bias, jnp.asarray(idx))


def _lam_kernel(q1_ref, k1_ref, q2_ref, k2_ref, o_ref, *, lam_init):
    e1 = jnp.exp(jnp.sum(q1_ref[...] * k1_ref[...], axis=-1, keepdims=True))
    e2 = jnp.exp(jnp.sum(q2_ref[...] * k2_ref[...], axis=-1, keepdims=True))
    o_ref[...] = jnp.broadcast_to(e1 - e2 + lam_init, o_ref.shape)


def _diff_lambda(lq1, lk1, lq2, lk2, lam_init):
    n = lq1.shape[0]
    out = pl.pallas_call(
        functools.partial(_lam_kernel, lam_init=lam_init),
        in_specs=[_full((1, n))] * 4,
        out_specs=_full((1, LANES)),
        out_shape=jax.ShapeDtypeStruct((1, LANES), F32),
        grid=(1,),
        compiler_params=_params(1),
        name="diff_lambda",
    )(lq1.reshape(1, n), lk1.reshape(1, n), lq2.reshape(1, n), lk2.reshape(1, n))
    return out[0, :1]


def _softmax_step(s, vb, m_ref, l_ref, acc_ref):
    m_prev = m_ref[...]
    m_new = jnp.maximum(m_prev, jnp.max(s, axis=1, keepdims=True))
    alpha = jnp.exp(m_prev - m_new)
    p = jnp.exp(s - m_new)
    l_ref[...] = alpha * l_ref[...] + jnp.sum(p, axis=1, keepdims=True)
    acc_ref[...] = alpha * acc_ref[...] + _bdot(p.astype(BF16), vb)
    m_ref[...] = m_new


def _flash_kernel(lam_ref, relb_ref, q_ref, k_ref, v_ref, b0_ref, b1_ref, g_ref, o_ref,
                  qs_ref, m_ref, l_ref, acc_ref, *, blk, lam_init):
    h = pl.program_id(1)
    i = pl.program_id(2)
    q = q_ref[0]
    lane = lax.broadcasted_iota(jnp.int32, q.shape, 1)
    zero = jnp.zeros_like(q)
    qs_ref[0:blk, :] = jnp.where(lane < ATT_HEAD_DIM, q, zero)
    qs_ref[blk:2 * blk, :] = jnp.where(lane >= ATT_HEAD_DIM, q, zero)
    m_ref[...] = jnp.full(m_ref.shape, NEG_INF, F32)
    l_ref[...] = jnp.zeros(l_ref.shape, F32)
    acc_ref[...] = jnp.zeros(acc_ref.shape, F32)

    def step(j, bias):
        rows = pl.ds(pl.multiple_of(j * blk, blk), blk)
        kb = k_ref[0, rows, :]
        vb = v_ref[0, rows, :]
        s = lax.dot_general(qs_ref[...], kb, (((1,), (1,)), ((), ())),
                            preferred_element_type=F32)
        if bias.ndim == 2:
            bias = jnp.concatenate([bias, bias], axis=0)
        _softmax_step(s + bias, vb, m_ref, l_ref, acc_ref)

    far_bias = relb_ref[N_BUCKETS - 1, h]

    def far_body(j, carry):
        step(j, far_bias)
        return carry

    lax.fori_loop(0, jnp.maximum(i - 1, 0), far_body, 0)

    @pl.when(i >= 1)
    def _():
        step(i - 1, b1_ref[0])

    step(i, b0_ref[0])

    o12 = acc_ref[...] / l_ref[...]
    o = o12[0:blk, :] - lam_ref[0] * o12[blk:2 * blk, :]
    y = _rms(o, g_ref[...], SUBLN_EPS) * (1.0 - lam_init)
    o_ref[0] = y.astype(BF16)


def _flash_attn(qb, kb, vb, lam, rel_bias, subln_g, lam_init):
    b, t, d = qb.shape
    blk = min(ATT_BLOCK, t)
    assert blk >= MAX_DISTANCE and t % blk == 0
    r = np.arange(blk)[:, None]
    c = np.arange(blk)[None, :]
    idx_diag = np.where(r >= c, _bucket_np(np.maximum(r - c, 0)), -1)
    idx_sub = _bucket_np(blk + r - c)
    tabs = _bias_tables(rel_bias, np.concatenate([idx_diag, idx_sub], axis=0).astype(np.int32))
    qspec = pl.BlockSpec((1, blk, ATT_V_DIM), lambda bi, h, i: (bi, i, h))
    kvspec = pl.BlockSpec((1, t, ATT_V_DIM), lambda bi, h, i: (bi, 0, h))
    return pl.pallas_call(
        functools.partial(_flash_kernel, blk=blk, lam_init=lam_init),
        grid=(b, ATT_HEADS, t // blk),
        in_specs=[pl.BlockSpec(memory_space=pltpu.SMEM),
                  pl.BlockSpec(memory_space=pltpu.SMEM),
                  qspec, kvspec, kvspec,
                  pl.BlockSpec((1, blk, blk), lambda bi, h, i: (h, 0, 0)),
                  pl.BlockSpec((1, blk, blk), lambda bi, h, i: (h, 1, 0)),
                  _full((1, ATT_V_DIM))],
        out_specs=qspec,
        out_shape=jax.ShapeDtypeStruct((b, t, d), BF16),
        scratch_shapes=[pltpu.VMEM((2 * blk, ATT_V_DIM), BF16),
                        pltpu.VMEM((2 * blk, 1), F32),
                        pltpu.VMEM((2 * blk, 1), F32),
                        pltpu.VMEM((2 * blk, ATT_V_DIM), F32)],
        compiler_params=_params(3),
        name="flash_diff_attn",
    )(lam, rel_bias, qb, kb, vb, tabs, tabs, subln_g.reshape(1, ATT_V_DIM))


def _decode_kernel(pt_ref, lam_ref, q_ref, ck_ref, cv_ref, kn_ref, vn_ref, tab_ref, g_ref,
                   o_ref, qbd_ref, m_ref, l_ref, acc_ref, *, tq, lam_init):
    del pt_ref
    p = pl.program_id(1)
    n_pages = pl.num_programs(1) - 1
    rows = 2 * ATT_HEADS * tq

    @pl.when(p == 0)
    def _():
        q = q_ref[0].astype(F32)
        qt = jnp.concatenate([q] * (2 * ATT_HEADS), axis=0)
        rid = lax.broadcasted_iota(jnp.int32, qt.shape, 0)
        lid = lax.broadcasted_iota(jnp.int32, qt.shape, 1)
        keep = (lid // ATT_HEAD_DIM) == (rid // tq)
        qbd_ref[...] = jnp.where(keep, qt, 0.0).astype(BF16)
        m_ref[...] = jnp.full(m_ref.shape, NEG_INF, F32)
        l_ref[...] = jnp.zeros(l_ref.shape, F32)
        acc_ref[...] = jnp.zeros(acc_ref.shape, F32)

    def step(kb, vb, bias):
        s = lax.dot_general(qbd_ref[...], kb, (((1,), (1,)), ((), ())),
                            preferred_element_type=F32)
        _softmax_step(s + bias, vb, m_ref, l_ref, acc_ref)

    @pl.when(p < n_pages - 1)
    def _():
        step(ck_ref[...].astype(BF16), cv_ref[...].astype(BF16), tab_ref[0])

    @pl.when(p == n_pages - 1)
    def _():
        step(ck_ref[...].astype(BF16), cv_ref[...].astype(BF16), tab_ref[1])

    @pl.when(p == n_pages)
    def _():
        step(kn_ref[0], vn_ref[0], tab_ref[2])
        o12 = acc_ref[...] / l_ref[...]
        for h in range(ATT_HEADS):
            blk = o12[2 * tq * h:2 * tq * (h + 1), ATT_V_DIM * h:ATT_V_DIM * (h + 1)]
            o = blk[0:tq, :] - lam_ref[0] * blk[tq:2 * tq, :]
            y = _rms(o, g_ref[...], SUBLN_EPS) * (1.0 - lam_init)
            o_ref[0, :, ATT_V_DIM * h:ATT_V_DIM * (h + 1)] = y
    del rows


def _decode_attn(qb, kb, vb, cache_k, cache_v, layer, page_table, lam, rel_bias, subln_g,
                 lam_init, t_valid):
    bs, tq, d = qb.shape
    n_pages = page_table.shape[1]
    ck = cache_k.reshape(cache_k.shape[0], cache_k.shape[1], PAGE_SIZE, d)
    cv = cache_v.reshape(cache_v.shape[0], cache_v.shape[1], PAGE_SIZE, d)
    kn = jnp.pad(kb, ((0, 0), (0, PAGE_SIZE - tq), (0, 0)))
    vn = jnp.pad(vb, ((0, 0), (0, PAGE_SIZE - tq), (0, 0)))
    tt = np.arange(tq)[:, None]
    cc = np.arange(PAGE_SIZE)[None, :]
    idx_far = np.full((tq, PAGE_SIZE), N_BUCKETS - 1)
    idx_last = _bucket_np(PAGE_SIZE - cc + tt)
    assert idx_last.min() >= 0 and (_bucket_np(PAGE_SIZE + 1 + 0 * cc + tt) == N_BUCKETS - 1).all()
    idx_new = np.where((cc <= tt) & (cc < t_valid), _bucket_np(np.maximum(tt - cc, 0)), -1)
    tabs = _bias_tables(rel_bias, np.concatenate([idx_far, idx_last, idx_new], 0).astype(np.int32))
    tabs = tabs.reshape(ATT_HEADS, 3, 1, tq, PAGE_SIZE)
    tabs = jnp.broadcast_to(tabs, (ATT_HEADS, 3, 2, tq, PAGE_SIZE))
    tabs = jnp.transpose(tabs, (1, 0, 2, 3, 4)).reshape(3, ATT_HEADS * 2 * tq, PAGE_SIZE)
    rows = ATT_HEADS * 2 * tq

    def page_map(bi, p, pt):
        return (layer, pt[bi, jnp.minimum(p, n_pages - 1)], 0, 0)

    seq = lambda bi, p, pt: (bi, 0, 0)
    grid_spec = pltpu.PrefetchScalarGridSpec(
        num_scalar_prefetch=1,
        grid=(bs, n_pages + 1),
        in_specs=[pl.BlockSpec(memory_space=pltpu.SMEM),
                  pl.BlockSpec((1, tq, d), seq),
                  pl.BlockSpec((None, None, PAGE_SIZE, d), page_map),
                  pl.BlockSpec((None, None, PAGE_SIZE, d), page_map),
                  pl.BlockSpec((1, PAGE_SIZE, d), seq),
                  pl.BlockSpec((1, PAGE_SIZE, d), seq),
                  pl.BlockSpec((3, rows, PAGE_SIZE), lambda bi, p, pt: (0, 0, 0)),
                  pl.BlockSpec((1, ATT_V_DIM), lambda bi, p, pt: (0, 0))],
        out_specs=pl.BlockSpec((1, tq, d), seq),
        scratch_shapes=[pltpu.VMEM((rows, d), BF16),
                        pltpu.VMEM((rows, 1), F32),
                        pltpu.VMEM((rows, 1), F32),
                        pltpu.VMEM((rows, d), F32)])
    return pl.pallas_call(
        functools.partial(_decode_kernel, tq=tq, lam_init=lam_init),
        grid_spec=grid_spec,
        out_shape=jax.ShapeDtypeStruct((bs, tq, d), F32),
        compiler_params=_params(2),
        name="decode_diff_attn",
    )(page_table, lam, qb, ck, cv, kn, vn, tabs, subln_g.reshape(1, ATT_V_DIM))


def _rwkv_pre_kernel(*refs, has_vres, t_last):
    if has_vres:
        (x_ref, sh_ref, gm_ref, mu_ref, wrkv_ref, w0_ref, w1_ref, w2_ref, a0_ref, a1_ref, a2_ref,
         g1_ref, g2_ref, vf_ref, v0_ref, v1_ref, v2_ref,
         r_ref, k_ref, v_ref, lw_ref, a_ref, g_ref, last_ref, carry_ref) = refs
    else:
        (x_ref, sh_ref, gm_ref, mu_ref, wrkv_ref, w0_ref, w1_ref, w2_ref, a0_ref, a1_ref, a2_ref,
         g1_ref, g2_ref,
         r_ref, k_ref, v_ref, lw_ref, a_ref, g_ref, last_ref, carry_ref) = refs
    ti = pl.program_id(1)
    tm = x_ref.shape[1]
    xn = _rms(x_ref[0], gm_ref[...], NORM_EPS)
    @pl.when(ti == 0)
    def _():
        carry_ref[...] = sh_ref[0]

    rolled = pltpu.roll(xn, 1, axis=0)
    rid = lax.broadcasted_iota(jnp.int32, xn.shape, 0)
    x_prev = jnp.where(rid == 0, carry_ref[...], rolled)
    carry_ref[...] = xn[tm - 1:tm, :]

    @pl.when(ti == pl.num_programs(1) - 1)
    def _():
        last_ref[0] = xn[t_last:t_last + 1, :]

    xx = x_prev - xn
    mix = lambda n: (xn + xx * mu_ref[n:n + 1, :]).astype(BF16)
    xr, xw, xk, xv, xa, xg = [mix(n) for n in range(6)]
    r_ref[0] = _bdot(xr, wrkv_ref[0])
    k_ref[0] = _bdot(xk, wrkv_ref[1])
    v = _bdot(xv, wrkv_ref[2])
    if has_vres:
        lo = _bdot(_bdot(xv, v1_ref[...]).astype(BF16), v2_ref[...])
        v = v + (vf_ref[0] - v) * _sigmoid(v0_ref[...] + lo)
    v_ref[0] = v
    wl = w0_ref[...] + _bdot(jnp.tanh(_bdot(xw, w1_ref[...])).astype(BF16), w2_ref[...])
    z = -wl
    softplus = jnp.maximum(z, 0.0) + jnp.log(1.0 + jnp.exp(-jnp.abs(z)))
    lw_ref[0] = -jnp.exp(-softplus - 0.5)
    a_ref[0] = _sigmoid(a0_ref[...] + _bdot(_bdot(xa, a1_ref[...]).astype(BF16), a2_ref[...]))
    g_ref[0] = _bdot(_sigmoid(_bdot(xg, g1_ref[...])).astype(BF16), g2_ref[...])


def _pad_lora(w_in, w_out):
    n = w_in.shape[1]
    n_pad = -(-n // LANES) * LANES - n
    return (jnp.pad(w_in, ((0, 0), (0, n_pad))).astype(BF16),
            jnp.pad(w_out, ((0, n_pad), (0, 0))).astype(BF16))


def _rwkv_pre(x3, shift, g_mix, mu, wrkv_bf, w0, w1, w2, a0, a1, a2, g1, g2, vres, v_first, t_valid):
    nseq, t, d = x3.shape
    tm = min(RWKV_TILE, t)
    nt = t // tm
    t_last = (t_valid - 1) - (nt - 1) * tm
    tile = pl.BlockSpec((1, tm, d), lambda s, i: (s, i, 0))
    vec = _full((1, d))
    w1p, w2p = _pad_lora(w1, w2)
    a1p, a2p = _pad_lora(a1, a2)
    g1p, g2p = _pad_lora(g1, g2)
    ins = [x3, shift.reshape(nseq, 1, d), g_mix.reshape(1, d), mu, wrkv_bf,
           w0.reshape(1, d), w1p, w2p, a0.reshape(1, d), a1p, a2p, g1p, g2p]
    specs = [tile, pl.BlockSpec((1, 1, d), lambda s, i: (s, 0, 0)), vec, _full(mu.shape),
             pl.BlockSpec(wrkv_bf.shape, lambda s, i: (0, 0, 0), pipeline_mode=pl.Buffered(1)),
             vec, _full(w1p.shape), _full(w2p.shape),
             vec, _full(a1p.shape), _full(a2p.shape), _full(g1p.shape), _full(g2p.shape)]
    if vres is not None:
        v0, v1, v2 = vres
        v1p, v2p = _pad_lora(v1, v2)
        ins += [v_first, v0.reshape(1, d), v1p, v2p]
        specs += [tile, vec, _full(v1p.shape), _full(v2p.shape)]
    big = jax.ShapeDtypeStruct((nseq, t, d), F32)
    outs = pl.pallas_call(
        functools.partial(_rwkv_pre_kernel, has_vres=vres is not None, t_last=t_last),
        grid=(nseq, nt),
        in_specs=specs,
        out_specs=[tile] * 6 + [pl.BlockSpec((1, 1, d), lambda s, i: (s, 0, 0))],
        out_shape=[big] * 6 + [jax.ShapeDtypeStruct((nseq, 1, d), F32)],
        scratch_shapes=[pltpu.VMEM((1, d), F32)],
        compiler_params=_params(2),
        name="rwkv_pre",
    )(*ins)
    return outs


def _wkv_prep_kernel(r_ref, k_ref, v_ref, lw_ref, a_ref, kk_ref, ka_ref, rk_ref,
                     rp_ref, yc_ref, bn_ref, g_ref, h_ref, *, t_valid):
    ln = r_ref.shape[1]
    d = r_ref.shape[2]
    n = RW_HEAD
    ri = lax.broadcasted_iota(jnp.int32, (ln, ln), 0)
    ci = lax.broadcasted_iota(jnp.int32, (ln, ln), 1)
    incl = (ri >= ci).astype(F32)
    strict = (ri > ci).astype(F32)
    eye_l = (ri == ci).astype(F32)
    eye_n = (lax.broadcasted_iota(jnp.int32, (n, n), 0)
             == lax.broadcasted_iota(jnp.int32, (n, n), 1)).astype(F32)
    if t_valid is None:
        live = None
    else:
        t_glob = pl.program_id(1) * ln + lax.broadcasted_iota(jnp.int32, (ln, 1), 0)
        live = (t_glob < t_valid).astype(F32)
    load = lambda ref, sl: ref[0, :, sl] if live is None else ref[0, :, sl] * live
    for h in range(d // n):
        sl = slice(h * n, (h + 1) * n)
        rh = load(r_ref, sl)
        kh = load(k_ref, sl)
        vh = load(v_ref, sl)
        ah = load(a_ref, sl)
        lw = load(lw_ref, sl)
        kk = kh * kk_ref[:, sl]
        kk = kk / jnp.maximum(jnp.sqrt(jnp.sum(kk * kk, axis=-1, keepdims=True)), 1e-12)
        kmod = kh * (1.0 + (ah - 1.0) * ka_ref[:, sl])
        bn_ref[0, :, sl] = jnp.sum(rh * kmod * rk_ref[:, sl], axis=-1, keepdims=True) * vh
        b_ = kk * ah
        cum = _hdot(incl, lw)
        cum_last = cum[ln - 1:ln, :]
        e_neg = jnp.exp(-cum)
        e_out = jnp.exp(cum_last - cum)
        at = -kk * jnp.exp(cum - lw)
        rt = rh * jnp.exp(cum)
        sc = _hdot_nt(jnp.concatenate([at, rt], axis=0),
                      jnp.concatenate([b_ * e_neg, kmod * e_neg], axis=0))
        ab = sc[0:ln, 0:ln] * strict
        ak = sc[0:ln, ln:2 * ln] * strict
        rb = sc[ln:2 * ln, 0:ln] * incl
        rk = sc[ln:2 * ln, ln:2 * ln] * incl
        tinv = eye_l + ab
        pw = ab
        for _ in range(int(math.log2(ln)) - 1):
            pw = _hdot(pw, pw)
            tinv = tinv + _hdot(pw, tinv)
        x = _hdot(tinv, jnp.concatenate([at, _hdot(ak, vh)], axis=1))
        ap = x[:, 0:n]
        c = x[:, n:2 * n]
        y = _hdot(rb, x)
        rp_ref[0, :, sl] = rt + y[:, 0:n]
        yc_ref[0, :, sl] = y[:, n:2 * n] + _hdot(rk, vh)
        bh = b_ * e_out
        g_ref[0, 0, :, sl] = eye_n * jnp.exp(cum_last) + _hdot_tn(ap, bh)
        h_ref[0, 0, :, sl] = _hdot_tn(c, bh) + _hdot_tn(vh, kmod * e_out)


def _wkv_scan_kernel(rp_ref, yc_ref, bn_ref, g_ref, h_ref, s0_ref, lnw_ref, lnb_ref,
                     z_ref, sf_ref, s_ref):
    n = RW_HEAD
    d = rp_ref.shape[2]

    @pl.when(pl.program_id(1) == 0)
    def _():
        s_ref[...] = s0_ref[0]

    for h in range(d // n):
        sl = slice(h * n, (h + 1) * n)
        s0 = s_ref[:, sl]
        y = _hdot_nt(rp_ref[0, :, sl], s0) + yc_ref[0, :, sl]
        s_ref[:, sl] = _hdot(s0, g_ref[0, 0, :, sl]) + h_ref[0, 0, :, sl]
        mean = jnp.mean(y, axis=-1, keepdims=True)
        yc = y - mean
        var = jnp.mean(yc * yc, axis=-1, keepdims=True)
        z_ref[0, :, sl] = (yc * lax.rsqrt(var + GN_EPS) * lnw_ref[:, sl] + lnb_ref[:, sl]
                           + bn_ref[0, :, sl])
    sf_ref[0] = s_ref[...]


def _wkv(r, k, v, lw, a, s0, k_k, k_a, r_k, ln_w, ln_b, t_valid):
    nseq, t, d = r.shape
    n = RW_HEAD
    ln = min(WKV_CHUNK, t)
    nc = t // ln
    tile = pl.BlockSpec((1, ln, d), lambda s, c: (s, c, 0))
    vec = _full((1, d))
    mat = pl.BlockSpec((1, 1, n, d), lambda s, c: (s, c, 0, 0))
    big = jax.ShapeDtypeStruct((nseq, t, d), F32)
    mats = jax.ShapeDtypeStruct((nseq, nc, n, d), F32)
    rp, yc, bn, g, hm = pl.pallas_call(
        functools.partial(_wkv_prep_kernel, t_valid=None if t_valid == t else t_valid),
        grid=(nseq, nc),
        in_specs=[tile] * 5 + [vec] * 3,
        out_specs=[tile] * 3 + [mat] * 2,
        out_shape=[big] * 3 + [mats] * 2,
        compiler_params=_params(2),
        name="wkv_prep",
    )(r, k, v, lw, a, k_k.reshape(1, d), k_a.reshape(1, d), r_k.reshape(1, d))
    s0_l = jnp.transpose(s0, (0, 2, 1, 3)).reshape(nseq, n, d)
    state = pl.BlockSpec((1, n, d), lambda s, c: (s, 0, 0))
    z, sf = pl.pallas_call(
        _wkv_scan_kernel,
        grid=(nseq, nc),
        in_specs=[tile] * 3 + [mat] * 2 + [state, vec, vec],
        out_specs=[tile, state],
        out_shape=[big, jax.ShapeDtypeStruct((nseq, n, d), F32)],
        scratch_shapes=[pltpu.VMEM((n, d), F32)],
        compiler_params=_params(2),
        name="wkv_scan",
    )(rp, yc, bn, g, hm, s0_l, ln_w.reshape(1, d), ln_b.reshape(1, d))
    return z, jnp.transpose(sf.reshape(nseq, n, d // n, n), (0, 2, 1, 3))


def kernel(x_prompt, x_sample, p_prompt, p_sample, cache_k, cache_v, state_wkv, state_shift, page_table, rel_bias, norm_mix, norm_mlp, norm_ple, norm_final, att_w_qkv, att_w_o, att_lam_q1, att_lam_k1, att_lam_q2, att_lam_k2, att_subln, rw_mu, rw_w_rkv, rw_w0, rw_w1, rw_w2, rw_a0, rw_a1, rw_a2, rw_v0, rw_v1, rw_v2, rw_g1, rw_g2, rw_k_k, rw_k_a, rw_r_k, rw_ln_w, rw_ln_b, rw_w_o, mlp_w1, mlp_w2, ple_w_in, ple_w_gate):
    depth = norm_mix.shape[0]
    bp, tp, d = x_prompt.shape
    bs, ts, _ = x_sample.shape
    ts_pad = -(-ts // SUBLANES) * SUBLANES
    heads_rw = d // RW_HEAD
    page_table = page_table.astype(jnp.int32)

    xp = x_prompt.reshape(bp * tp, d)
    xs = jnp.pad(x_sample, ((0, 0), (0, ts_pad - ts), (0, 0))).reshape(bs * ts_pad, d)
    pp_all = p_prompt.reshape(depth, bp * tp, -1)
    ps_all = jnp.pad(p_sample, ((0, 0), (0, 0), (0, ts_pad - ts), (0, 0))).reshape(depth, bs * ts_pad, -1)

    kp_l, vp_l, ks_l, vs_l = [], [], [], []
    wkvp_l, shp_l, wkvs_l, shs_l = [], [], [], []
    vf_p = vf_s = None
    y_p = y_s = None
    for i in range(depth):
        if i % 2 == 0:
            a = i // 2
            lam_init = 0.8 - 0.6 * math.exp(-0.3 * i)
            lam = _diff_lambda(att_lam_q1[a], att_lam_k1[a], att_lam_q2[a], att_lam_k2[a], lam_init)
            wqkv = att_w_qkv[a].astype(BF16)
            wo = att_w_o[a].astype(BF16)
            qb, k, v, kb, vb = _qkv_proj(xp, norm_mix[i], wqkv)
            o = _flash_attn(qb.reshape(bp, tp, d), kb.reshape(bp, tp, d), vb.reshape(bp, tp, d),
                            lam, rel_bias, att_subln[a], lam_init)
            xp = _proj_res(o.reshape(bp * tp, d), xp, wo)
            kp_l.append(k.reshape(bp, tp, ATT_HEADS, ATT_V_DIM))
            vp_l.append(v.reshape(bp, tp, ATT_HEADS, ATT_V_DIM))

            qb, k, v, kb, vb = _qkv_proj(xs, norm_mix[i], wqkv)
            o = _decode_attn(qb.reshape(bs, ts_pad, d), kb.reshape(bs, ts_pad, d),
                             vb.reshape(bs, ts_pad, d), cache_k, cache_v, a, page_table, lam,
                             rel_bias, att_subln[a], lam_init, ts)
            xs = _proj_res(o.reshape(bs * ts_pad, d), xs, wo)
            ks_l.append(k.reshape(bs, ts_pad, ATT_HEADS, ATT_V_DIM)[:, :ts])
            vs_l.append(v.reshape(bs, ts_pad, ATT_HEADS, ATT_V_DIM)[:, :ts])
        else:
            r = i // 2
            vres = None if r == 0 else (rw_v0[r - 1], rw_v1[r - 1], rw_v2[r - 1])
            wrkv = rw_w_rkv[r].astype(BF16)
            wo = rw_w_o[r].astype(BF16)
            lora = (rw_w0[r], rw_w1[r], rw_w2[r], rw_a0[r], rw_a1[r], rw_a2[r], rw_g1[r], rw_g2[r])
            heads = (rw_k_k[r], rw_k_a[r], rw_r_k[r].reshape(d), rw_ln_w[r], rw_ln_b[r])

            def time_mix(x2, nseq, t, shift, s0, v_first, t_valid):
                rr, kk, vv, lw, aa, gg, last = _rwkv_pre(
                    x2.reshape(nseq, t, d), shift, norm_mix[i], rw_mu[r], wrkv, *lora, vres,
                    v_first, t_valid)
                z, s_fin = _wkv(rr, kk, vv, lw, aa, s0, *heads, t_valid)
                out = _proj_res(z.reshape(nseq * t, d), x2, wo, gate2=gg.reshape(nseq * t, d))
                return out, s_fin, last.reshape(nseq, d), (vv if vres is None else v_first)

            xp, s_p, last_p, vf_p = time_mix(
                xp, bp, tp, jnp.zeros((bp, d), F32),
                jnp.zeros((bp, heads_rw, RW_HEAD, RW_HEAD), F32), vf_p, tp)
            xs, s_s, last_s, vf_s = time_mix(xs, bs, ts_pad, state_shift[r], state_wkv[r], vf_s, ts)
            wkvp_l.append(s_p)
            shp_l.append(last_p)
            wkvs_l.append(s_s)
            shs_l.append(last_s)
        final = i == depth - 1
        ffn_w = (norm_mlp[i], mlp_w1[i].astype(BF16), mlp_w2[i].astype(BF16), norm_ple[i],
                 ple_w_in[i].astype(BF16), ple_w_gate[i].astype(BF16), norm_final)
        xp, y_p = _ffn_ple(xp, pp_all[i], *ffn_w, final)
        xs, y_s = _ffn_ple(xs, ps_all[i], *ffn_w, final)

    y_prompt = y_p.reshape(bp, tp, d)
    y_sample = y_s.reshape(bs, ts_pad, d)[:, :ts]
    return (y_prompt, y_sample, jnp.stack(kp_l), jnp.stack(vp_l), jnp.stack(wkvp_l),
            jnp.stack(shp_l), jnp.stack(ks_l), jnp.stack(vs_l), jnp.stack(wkvs_l),
            jnp.stack(shs_l))
```

```python
import functools
import math

import numpy as np
import jax
import jax.numpy as jnp
from jax import lax
from jax.experimental import pallas as pl
from jax.experimental.pallas import tpu as pltpu

F32 = jnp.float32
BF16 = jnp.bfloat16
HIGHEST = lax.Precision.HIGHEST

ATT_HEADS = 8
ATT_HEAD_DIM = 64
ATT_V_DIM = 128
N_BUCKETS = 32
MAX_EXACT = 16
MAX_DISTANCE = 128
PAGE_SIZE = 128
RW_HEAD = 64
NORM_EPS = 1e-6
SUBLN_EPS = 1e-5
GN_EPS = 64e-5
NEG_INF = -1e30

LANES = 128
SUBLANES = 8
VMEM_LIMIT = 56 * 1024 * 1024
TOKEN_TILE = 512
RWKV_TILE = 256
ATT_BLOCK = 512
ATT_ROW_CHUNK = 128
WKV_CHUNK = 64
FF_CHUNK = 1024

NN = (((1,), (0,)), ((), ()))
NT = (((1,), (1,)), ((), ()))
TN = (((0,), (0,)), ((), ()))


def _params(n_axes):
    return pltpu.CompilerParams(dimension_semantics=("arbitrary",) * n_axes,
                                vmem_limit_bytes=VMEM_LIMIT)


def _rms(x, g, eps):
    return x * lax.rsqrt(jnp.mean(x * x, axis=-1, keepdims=True) + eps) * g


def _sigmoid(x):
    return 1.0 / (1.0 + jnp.exp(-x))


def _bdot(a, b):
    return jnp.dot(a, b, preferred_element_type=F32)


def _split(a):
    hi = a.astype(BF16)
    return hi, (a - hi.astype(F32)).astype(BF16)


def _dot3(a, b, dims=NN):
    ah, al = _split(a)
    bh, bl = _split(b)
    d = lambda x, y: lax.dot_general(x, y, dims, preferred_element_type=F32)
    return d(ah, bh) + (d(ah, bl) + d(al, bh))


def _full(shape):
    zeros = (0,) * len(shape)
    return pl.BlockSpec(shape, lambda *_: zeros)


def _qkv_kernel(x_ref, g_ref, w_ref, qb_ref, k_ref, v_ref, kb_ref, vb_ref):
    d = x_ref.shape[1]
    xn = _rms(x_ref[...], g_ref[...], NORM_EPS).astype(BF16)
    q = _bdot(xn, w_ref[:, 0:d])
    qb_ref[...] = (q * (ATT_HEAD_DIM ** -0.5)).astype(BF16)
    k = _bdot(xn, w_ref[:, d:2 * d])
    k_ref[...] = k
    kb_ref[...] = k.astype(BF16)
    v = _bdot(xn, w_ref[:, 2 * d:3 * d])
    v_ref[...] = v
    vb_ref[...] = v.astype(BF16)


def _qkv_proj(x2, g, w_bf):
    m, d = x2.shape
    tm = min(TOKEN_TILE, m)
    row = pl.BlockSpec((tm, d), lambda i: (i, 0))
    return pl.pallas_call(
        _qkv_kernel,
        grid=(m // tm,),
        in_specs=[row, _full((1, d)), _full((d, 3 * d))],
        out_specs=[row, row, row, row, row],
        out_shape=[jax.ShapeDtypeStruct((m, d), BF16),
                   jax.ShapeDtypeStruct((m, d), F32),
                   jax.ShapeDtypeStruct((m, d), F32),
                   jax.ShapeDtypeStruct((m, d), BF16),
                   jax.ShapeDtypeStruct((m, d), BF16)],
        compiler_params=_params(1),
        name="qkv_proj",
    )(x2, g.reshape(1, d), w_bf)


def _proj_res_kernel(a_ref, res_ref, w_ref, o_ref):
    o_ref[...] = res_ref[...] + _bdot(a_ref[...].astype(BF16), w_ref[...])


def _proj_gate_res_kernel(a_ref, gate_ref, res_ref, w_ref, o_ref):
    a = (a_ref[...] * gate_ref[...]).astype(BF16)
    o_ref[...] = res_ref[...] + _bdot(a, w_ref[...])


def _proj_res(a2, res2, w_bf, gate2=None):
    m, d = res2.shape
    tm = min(TOKEN_TILE, m)
    row = pl.BlockSpec((tm, d), lambda i: (i, 0))
    if gate2 is None:
        kern, ins, specs = _proj_res_kernel, (a2, res2, w_bf), [row, row, _full((d, d))]
    else:
        kern, ins = _proj_gate_res_kernel, (a2, gate2, res2, w_bf)
        specs = [row, row, row, _full((d, d))]
    return pl.pallas_call(
        kern,
        grid=(m // tm,),
        in_specs=specs,
        out_specs=row,
        out_shape=jax.ShapeDtypeStruct((m, d), F32),
        compiler_params=_params(1),
        name="proj_res",
    )(*ins)


def _ffn_kernel(x_ref, p_ref, gm_ref, w1_ref, w2_ref, gp_ref, win_ref, wg_ref, gf_ref,
                o_ref, *y_ref):
    x = x_ref[...]
    h = _rms(x, gm_ref[...], NORM_EPS).astype(BF16)
    acc = x
    d_ff = w1_ref.shape[1]
    for c in range(d_ff // FF_CHUNK):
        cols = slice(c * FF_CHUNK, (c + 1) * FF_CHUNK)
        a = jnp.maximum(_bdot(h, w1_ref[:, cols]), 0.0)
        acc = acc + _bdot((a * a).astype(BF16), w2_ref[cols, :])
    gate = _sigmoid(_bdot(_rms(acc, gp_ref[...], NORM_EPS).astype(BF16), wg_ref[...]))
    out = acc + _bdot(p_ref[...].astype(BF16), win_ref[...]) * gate
    o_ref[...] = out
    if y_ref:
        y_ref[0][...] = _rms(out, gf_ref[...], NORM_EPS)


def _ffn_ple(x2, p2, g_mlp, w1_bf, w2_bf, g_ple, win_bf, wg_bf, g_final, emit_final):
    m, d = x2.shape
    pd = p2.shape[1]
    d_ff = w1_bf.shape[1]
    tm = min(TOKEN_TILE, m)
    row = pl.BlockSpec((tm, d), lambda i: (i, 0))
    once = dict(pipeline_mode=pl.Buffered(1))
    specs = [row, pl.BlockSpec((tm, pd), lambda i: (i, 0)), _full((1, d)),
             pl.BlockSpec((d, d_ff), lambda i: (0, 0), **once),
             pl.BlockSpec((d_ff, d), lambda i: (0, 0), **once),
             _full((1, d)),
             pl.BlockSpec((pd, d), lambda i: (0, 0), **once),
             pl.BlockSpec((d, d), lambda i: (0, 0), **once),
             _full((1, d))]
    n_out = 2 if emit_final else 1
    out = pl.pallas_call(
        _ffn_kernel,
        grid=(m // tm,),
        in_specs=specs,
        out_specs=[row] * n_out,
        out_shape=[jax.ShapeDtypeStruct((m, d), F32)] * n_out,
        compiler_params=_params(1),
        name="ffn_ple",
    )(x2, p2, g_mlp.reshape(1, d), w1_bf, w2_bf, g_ple.reshape(1, d), win_bf, wg_bf,
      g_final.reshape(1, d))
    return out if emit_final else (out[0], None)


def _bucket_np(n):
    n = np.asarray(n, np.int64)
    nf = np.maximum(n, 1).astype(np.float32)
    large = MAX_EXACT + (np.log(nf / np.float32(MAX_EXACT))
                         / np.float32(math.log(MAX_DISTANCE / MAX_EXACT))
                         * np.float32(N_BUCKETS - MAX_EXACT)).astype(np.int32)
    large = np.minimum(large, N_BUCKETS - 1)
    return np.where(n < MAX_EXACT, n, large).astype(np.int32)


def _bias_kernel(relb_ref, idx_ref, o_ref):
    h = pl.program_id(0)
    idx = idx_ref[...]
    acc = jnp.full(idx.shape, NEG_INF, F32)
    for b in range(N_BUCKETS):
        acc = jnp.where(idx == b, relb_ref[b, h], acc)
    o_ref[0] = acc


def _bias_tables(rel_bias, idx):
    r, c = idx.shape
    return pl.pallas_call(
        _bias_kernel,
        grid=(ATT_HEADS,),
        in_specs=[pl.BlockSpec(memory_space=pltpu.SMEM), _full((r, c))],
        out_specs=pl.BlockSpec((1, r, c), lambda h: (h, 0, 0)),
        out_shape=jax.ShapeDtypeStruct((ATT_HEADS, r, c), F32),
        compiler_params=_params(1),
        name="bias_tables",
    )(rel_bias, jnp.asarray(idx.astype(np.int32)))


def _lam_kernel(q1_ref, k1_ref, q2_ref, k2_ref, o_ref, *, lam_init):
    e1 = jnp.exp(jnp.sum(q1_ref[...] * k1_ref[...], axis=-1, keepdims=True))
    e2 = jnp.exp(jnp.sum(q2_ref[...] * k2_ref[...], axis=-1, keepdims=True))
    o_ref[...] = jnp.broadcast_to(e1 - e2 + lam_init, o_ref.shape)


def _diff_lambda(lq1, lk1, lq2, lk2, lam_init):
    n = lq1.shape[0]
    out = pl.pallas_call(
        functools.partial(_lam_kernel, lam_init=lam_init),
        in_specs=[_full((1, n))] * 4,
        out_specs=_full((1, LANES)),
        out_shape=jax.ShapeDtypeStruct((1, LANES), F32),
        grid=(1,),
        compiler_params=_params(1),
        name="diff_lambda",
    )(lq1.reshape(1, n), lk1.reshape(1, n), lq2.reshape(1, n), lk2.reshape(1, n))
    return out[0, :1]


def _online_softmax(s, v1, m_ref, acc_ref, rows):
    tiles = [s[:, c * LANES:(c + 1) * LANES] for c in range(s.shape[1] // LANES)]
    m_prev = m_ref[rows, :]
    row_max = jnp.max(functools.reduce(jnp.maximum, tiles), axis=1, keepdims=True)
    m_new = jnp.maximum(m_prev, row_max)
    alpha = jnp.exp(m_prev - m_new)
    p = jnp.concatenate([jnp.exp(t - m_new).astype(BF16) for t in tiles], axis=1)
    acc_ref[rows, :] = jnp.concatenate([alpha, alpha], axis=1) * acc_ref[rows, :] + _bdot(p, v1)
    m_ref[rows, :] = m_new


def _flash_kernel(lam_ref, relb_ref, q_ref, k_ref, v_ref, b0_ref, b1_ref, g_ref, o_ref,
                  qs_ref, v1_ref, m_ref, acc_ref, *, blk, chunk, lam_init):
    h = pl.program_id(1)
    i = pl.program_id(2)

    @pl.when(i == 0)
    def _():
        v1_ref[:, 0:ATT_V_DIM] = v_ref[0]
        v1_ref[:, ATT_V_DIM:2 * ATT_V_DIM] = jnp.ones((v1_ref.shape[0], ATT_V_DIM), BF16)

    q = q_ref[0]
    lane = lax.broadcasted_iota(jnp.int32, q.shape, 1)
    zero = jnp.zeros_like(q)
    qs_ref[0:blk, :] = jnp.where(lane < ATT_HEAD_DIM, q, zero)
    qs_ref[blk:2 * blk, :] = jnp.where(lane >= ATT_HEAD_DIM, q, zero)
    m_ref[...] = jnp.full(m_ref.shape, NEG_INF, F32)
    acc_ref[...] = jnp.zeros(acc_ref.shape, F32)

    def step(j, bias):
        cols = pl.ds(pl.multiple_of(j * blk, blk), blk)
        kb = k_ref[0, cols, :]
        v1 = v1_ref[cols, :]
        for c in range(2 * blk // chunk):
            rows = slice(c * chunk, (c + 1) * chunk)
            s = lax.dot_general(qs_ref[rows, :], kb, NT, preferred_element_type=F32)
            if bias.ndim == 2:
                r0 = (c * chunk) % blk
                s = s + bias[r0:r0 + chunk, :]
            else:
                s = s + bias
            _online_softmax(s, v1, m_ref, acc_ref, rows)

    far_bias = relb_ref[N_BUCKETS - 1, h]

    def far_body(j, carry):
        step(j, far_bias)
        return carry

    lax.fori_loop(0, jnp.maximum(i - 1, 0), far_body, 0)

    @pl.when(i >= 1)
    def _():
        step(i - 1, b1_ref[0])

    step(i, b0_ref[0])

    acc = acc_ref[...]
    o12 = acc[:, 0:ATT_V_DIM] / acc[:, ATT_V_DIM:2 * ATT_V_DIM]
    o = o12[0:blk, :] - lam_ref[0] * o12[blk:2 * blk, :]
    y = _rms(o, g_ref[...], SUBLN_EPS) * (1.0 - lam_init)
    o_ref[0] = y.astype(BF16)


def _flash_bias_tables(rel_bias, t):
    blk = min(ATT_BLOCK, t)
    assert blk >= MAX_DISTANCE and t % blk == 0
    r = np.arange(blk)[:, None]
    c = np.arange(blk)[None, :]
    idx_diag = np.where(r >= c, _bucket_np(np.maximum(r - c, 0)), -1)
    idx_sub = _bucket_np(blk + r - c)
    assert (_bucket_np(np.arange(blk + 1, 4 * blk)) == N_BUCKETS - 1).all()
    return _bias_tables(rel_bias, np.concatenate([idx_diag, idx_sub], axis=0))


def _flash_attn(qb, kb, vb, lam, rel_bias, tabs, subln_g, lam_init):
    b, t, d = qb.shape
    blk = min(ATT_BLOCK, t)
    qspec = pl.BlockSpec((1, blk, ATT_V_DIM), lambda bi, h, i: (bi, i, h))
    kvspec = pl.BlockSpec((1, t, ATT_V_DIM), lambda bi, h, i: (bi, 0, h))
    return pl.pallas_call(
        functools.partial(_flash_kernel, blk=blk, chunk=min(ATT_ROW_CHUNK, blk),
                          lam_init=lam_init),
        grid=(b, ATT_HEADS, t // blk),
        in_specs=[pl.BlockSpec(memory_space=pltpu.SMEM),
                  pl.BlockSpec(memory_space=pltpu.SMEM),
                  qspec, kvspec, kvspec,
                  pl.BlockSpec((1, blk, blk), lambda bi, h, i: (h, 0, 0)),
                  pl.BlockSpec((1, blk, blk), lambda bi, h, i: (h, 1, 0)),
                  _full((1, ATT_V_DIM))],
        out_specs=qspec,
        out_shape=jax.ShapeDtypeStruct((b, t, d), BF16),
        scratch_shapes=[pltpu.VMEM((2 * blk, ATT_V_DIM), BF16),
                        pltpu.VMEM((t, 2 * ATT_V_DIM), BF16),
                        pltpu.VMEM((2 * blk, LANES), F32),
                        pltpu.VMEM((2 * blk, 2 * ATT_V_DIM), F32)],
        compiler_params=_params(3),
        name="flash_diff_attn",
    )(lam, rel_bias, qb, kb, vb, tabs, tabs, subln_g.reshape(1, ATT_V_DIM))


def _decode_kernel(pt_ref, lam_ref, q_ref, ck_ref, cv_ref, kn_ref, vn_ref, tab_ref, g_ref,
                   o_ref, qa_ref, m_ref, l_ref, acc_ref, *, tq, lam_init):
    del pt_ref
    p = pl.program_id(1)
    n_pages = pl.num_programs(1) - 1
    grp = 2 * tq
    n_cols = PAGE_SIZE * ATT_HEADS

    @pl.when(p == 0)
    def _():
        for h in range(ATT_HEADS):
            qh = q_ref[0, :, ATT_V_DIM * h:ATT_V_DIM * (h + 1)].astype(F32)
            lane = lax.broadcasted_iota(jnp.int32, qh.shape, 1)
            qa_ref[h * grp:h * grp + tq, :] = jnp.where(lane < ATT_HEAD_DIM, qh, 0.0)
            qa_ref[h * grp + tq:(h + 1) * grp, :] = jnp.where(lane >= ATT_HEAD_DIM, qh, 0.0)
        m_ref[...] = jnp.full(m_ref.shape, NEG_INF, F32)
        l_ref[...] = jnp.zeros(l_ref.shape, F32)
        acc_ref[...] = jnp.zeros(acc_ref.shape, F32)

    def step(k3, v3, tab):
        k2 = k3.reshape(n_cols, ATT_V_DIM).astype(BF16)
        v2 = v3.reshape(n_cols, ATT_V_DIM).astype(BF16)
        s = lax.dot_general(qa_ref[...].astype(BF16), k2, NT, preferred_element_type=F32) + tab
        tiles = [s[:, c * LANES:(c + 1) * LANES] for c in range(n_cols // LANES)]
        m_prev = m_ref[...]
        row_max = jnp.max(functools.reduce(jnp.maximum, tiles), axis=1, keepdims=True)
        m_new = jnp.maximum(m_prev, row_max)
        alpha = jnp.exp(m_prev - m_new)
        ps = [jnp.exp(t - m_new) for t in tiles]
        row_sum = jnp.sum(functools.reduce(jnp.add, ps), axis=1, keepdims=True)
        l_ref[...] = alpha * l_ref[...] + row_sum
        pb = jnp.concatenate([x.astype(BF16) for x in ps], axis=1)
        acc_ref[...] = alpha * acc_ref[...] + _bdot(pb, v2)
        m_ref[...] = m_new

    @pl.when(p < n_pages - 1)
    def _():
        step(ck_ref[...], cv_ref[...], tab_ref[0])

    @pl.when(p == n_pages - 1)
    def _():
        step(ck_ref[...], cv_ref[...], tab_ref[1])

    @pl.when(p == n_pages)
    def _():
        step(kn_ref[0], vn_ref[0], tab_ref[2])
        o12 = acc_ref[...] / l_ref[...]
        for h in range(ATT_HEADS):
            o = o12[h * grp:h * grp + tq, :] - lam_ref[0] * o12[h * grp + tq:(h + 1) * grp, :]
            y = _rms(o, g_ref[...], SUBLN_EPS) * (1.0 - lam_init)
            o_ref[0, :, ATT_V_DIM * h:ATT_V_DIM * (h + 1)] = y


def _decode_bias_tables(rel_bias, tq, t_valid):
    tt = np.arange(tq)[:, None]
    cc = np.arange(PAGE_SIZE)[None, :]
    idx_far = np.full((tq, PAGE_SIZE), N_BUCKETS - 1)
    idx_last = _bucket_np(PAGE_SIZE - cc + tt)
    assert (_bucket_np(PAGE_SIZE + 1 + 0 * cc + tt) == N_BUCKETS - 1).all()
    idx_new = np.where((cc <= tt) & (cc < t_valid), _bucket_np(np.maximum(tt - cc, 0)), -1)
    tabs = _bias_tables(rel_bias, np.concatenate([idx_far, idx_last, idx_new], axis=0))
    h = ATT_HEADS
    tb = jnp.transpose(tabs.reshape(h, 3, tq, PAGE_SIZE), (1, 0, 2, 3))
    tb = jnp.broadcast_to(tb[:, :, None, :, :, None], (3, h, 2, tq, PAGE_SIZE, h))
    own = np.eye(h, dtype=bool)[None, :, None, None, None, :]
    return jnp.where(own, tb, NEG_INF).reshape(3, h * 2 * tq, PAGE_SIZE * h)


def _decode_attn(qb, k, v, cache_k, cache_v, layer, page_table, lam, tabs, subln_g, lam_init):
    bs, tq, d = qb.shape
    n_pages = page_table.shape[1]
    heads = (ATT_HEADS, ATT_V_DIM)
    pad = ((0, 0), (0, PAGE_SIZE - tq), (0, 0), (0, 0))
    kn = jnp.pad(k.reshape(bs, tq, *heads), pad)
    vn = jnp.pad(v.reshape(bs, tq, *heads), pad)
    rows = ATT_HEADS * 2 * tq

    def page_map(bi, p, pt):
        return (layer, pt[bi, jnp.minimum(p, n_pages - 1)], 0, 0, 0)

    seq3 = lambda bi, p, pt: (bi, 0, 0)
    seq4 = lambda bi, p, pt: (bi, 0, 0, 0)
    page = pl.BlockSpec((None, None, PAGE_SIZE) + heads, page_map)
    new = pl.BlockSpec((1, PAGE_SIZE) + heads, seq4)
    grid_spec = pltpu.PrefetchScalarGridSpec(
        num_scalar_prefetch=1,
        grid=(bs, n_pages + 1),
        in_specs=[pl.BlockSpec(memory_space=pltpu.SMEM),
                  pl.BlockSpec((1, tq, d), seq3),
                  page, page, new, new,
                  pl.BlockSpec(tabs.shape, lambda bi, p, pt: (0, 0, 0)),
                  pl.BlockSpec((1, ATT_V_DIM), lambda bi, p, pt: (0, 0))],
        out_specs=pl.BlockSpec((1, tq, d), seq3),
        scratch_shapes=[pltpu.VMEM((rows, ATT_V_DIM), F32),
                        pltpu.VMEM((rows, LANES), F32),
                        pltpu.VMEM((rows, LANES), F32),
                        pltpu.VMEM((rows, ATT_V_DIM), F32)])
    return pl.pallas_call(
        functools.partial(_decode_kernel, tq=tq, lam_init=lam_init),
        grid_spec=grid_spec,
        out_shape=jax.ShapeDtypeStruct((bs, tq, d), F32),
        compiler_params=_params(2),
        name="decode_diff_attn",
    )(page_table, lam, qb, cache_k, cache_v, kn, vn, tabs, subln_g.reshape(1, ATT_V_DIM))


def _rwkv_pre_kernel(*refs, has_vres, t_last):
    if has_vres:
        (x_ref, sh_ref, gm_ref, mu_ref, wrkv_ref, w0_ref, w1_ref, w2_ref, a0_ref, a1_ref, a2_ref,
         g1_ref, g2_ref, vf_ref, v0_ref, v1_ref, v2_ref,
         r_ref, k_ref, v_ref, lw_ref, a_ref, g_ref, last_ref, carry_ref) = refs
    else:
        (x_ref, sh_ref, gm_ref, mu_ref, wrkv_ref, w0_ref, w1_ref, w2_ref, a0_ref, a1_ref, a2_ref,
         g1_ref, g2_ref,
         r_ref, k_ref, v_ref, lw_ref, a_ref, g_ref, last_ref, carry_ref) = refs
    ti = pl.program_id(1)
    tm = x_ref.shape[1]
    xn = _rms(x_ref[0], gm_ref[...], NORM_EPS)

    @pl.when(ti == 0)
    def _():
        carry_ref[...] = sh_ref[0]

    rolled = pltpu.roll(xn, 1, axis=0)
    rid = lax.broadcasted_iota(jnp.int32, xn.shape, 0)
    x_prev = jnp.where(rid == 0, carry_ref[...], rolled)
    carry_ref[...] = xn[tm - 1:tm, :]

    @pl.when(ti == pl.num_programs(1) - 1)
    def _():
        last_ref[0] = xn[t_last:t_last + 1, :]

    xx = x_prev - xn
    mix = lambda n: (xn + xx * mu_ref[n:n + 1, :]).astype(BF16)
    xr, xw, xk, xv, xa, xg = [mix(n) for n in range(6)]
    r_ref[0] = _bdot(xr, wrkv_ref[0])
    k_ref[0] = _bdot(xk, wrkv_ref[1])
    v = _bdot(xv, wrkv_ref[2])
    if has_vres:
        lo = _bdot(_bdot(xv, v1_ref[...]).astype(BF16), v2_ref[...])
        v = v + (vf_ref[0] - v) * _sigmoid(v0_ref[...] + lo)
    v_ref[0] = v
    wl = w0_ref[...] + _bdot(jnp.tanh(_bdot(xw, w1_ref[...])).astype(BF16), w2_ref[...])
    z = -wl
    softplus = jnp.maximum(z, 0.0) + jnp.log(1.0 + jnp.exp(-jnp.abs(z)))
    lw_ref[0] = -jnp.exp(-softplus - 0.5)
    a_ref[0] = _sigmoid(a0_ref[...] + _bdot(_bdot(xa, a1_ref[...]).astype(BF16), a2_ref[...]))
    g_ref[0] = _bdot(_sigmoid(_bdot(xg, g1_ref[...])).astype(BF16), g2_ref[...])


def _pad_lora(w_in, w_out):
    n = w_in.shape[1]
    n_pad = -(-n // LANES) * LANES - n
    return (jnp.pad(w_in, ((0, 0), (0, n_pad))).astype(BF16),
            jnp.pad(w_out, ((0, n_pad), (0, 0))).astype(BF16))


def _rwkv_pre(x3, shift, g_mix, mu, wrkv_bf, w0, w1, w2, a0, a1, a2, g1, g2, vres, v_first, t_valid):
    nseq, t, d = x3.shape
    tm = min(RWKV_TILE, t)
    nt = t // tm
    t_last = (t_valid - 1) - (nt - 1) * tm
    tile = pl.BlockSpec((1, tm, d), lambda s, i: (s, i, 0))
    vec = _full((1, d))
    w1p, w2p = _pad_lora(w1, w2)
    a1p, a2p = _pad_lora(a1, a2)
    g1p, g2p = _pad_lora(g1, g2)
    ins = [x3, shift.reshape(nseq, 1, d), g_mix.reshape(1, d), mu, wrkv_bf,
           w0.reshape(1, d), w1p, w2p, a0.reshape(1, d), a1p, a2p, g1p, g2p]
    specs = [tile, pl.BlockSpec((1, 1, d), lambda s, i: (s, 0, 0)), vec, _full(mu.shape),
             pl.BlockSpec(wrkv_bf.shape, lambda s, i: (0, 0, 0), pipeline_mode=pl.Buffered(1)),
             vec, _full(w1p.shape), _full(w2p.shape),
             vec, _full(a1p.shape), _full(a2p.shape), _full(g1p.shape), _full(g2p.shape)]
    if vres is not None:
        v0, v1, v2 = vres
        v1p, v2p = _pad_lora(v1, v2)
        ins += [v_first, v0.reshape(1, d), v1p, v2p]
        specs += [tile, vec, _full(v1p.shape), _full(v2p.shape)]
    big = jax.ShapeDtypeStruct((nseq, t, d), F32)
    outs = pl.pallas_call(
        functools.partial(_rwkv_pre_kernel, has_vres=vres is not None, t_last=t_last),
        grid=(nseq, nt),
        in_specs=specs,
        out_specs=[tile] * 6 + [pl.BlockSpec((1, 1, d), lambda s, i: (s, 0, 0))],
        out_shape=[big] * 6 + [jax.ShapeDtypeStruct((nseq, 1, d), F32)],
        scratch_shapes=[pltpu.VMEM((1, d), F32)],
        compiler_params=_params(2),
        name="rwkv_pre",
    )(*ins)
    return outs


def _wkv_prep_kernel(r_ref, k_ref, v_ref, lw_ref, a_ref, kk_ref, ka_ref, rk_ref,
                     rp_ref, yc_ref, bn_ref, g_ref, h_ref, *, t_valid):
    ln = r_ref.shape[1]
    d = r_ref.shape[2]
    n = RW_HEAD
    heads = range(d // n)
    sl = [slice(h * n, (h + 1) * n) for h in heads]
    ri = lax.broadcasted_iota(jnp.int32, (ln, ln), 0)
    ci = lax.broadcasted_iota(jnp.int32, (ln, ln), 1)
    incl = (ri >= ci).astype(F32)
    strict = (ri > ci).astype(F32)
    eye_l = (ri == ci).astype(F32)
    eye_n = (lax.broadcasted_iota(jnp.int32, (n, n), 0)
             == lax.broadcasted_iota(jnp.int32, (n, n), 1)).astype(F32)
    r, k, v, lw, a = r_ref[0], k_ref[0], v_ref[0], lw_ref[0], a_ref[0]
    if t_valid is not None:
        t_glob = pl.program_id(1) * ln + lax.broadcasted_iota(jnp.int32, (ln, 1), 0)
        live = (t_glob < t_valid).astype(F32)
        r, k, v, lw, a = r * live, k * live, v * live, lw * live, a * live

    cum = jnp.dot(incl, lw, precision=HIGHEST, preferred_element_type=F32)
    cum_last = cum[ln - 1:ln, :]
    e_neg = jnp.exp(-cum)
    e_out = jnp.exp(cum_last - cum)
    e_prev = jnp.exp(cum - lw)
    w_last = jnp.exp(cum_last)
    rt = r * jnp.exp(cum)
    kmod = k * (1.0 + (a - 1.0) * ka_ref[...])
    kt = kmod * e_neg
    ko = kmod * e_out
    kk0 = k * kk_ref[...]
    rkr = r * kmod * rk_ref[...]

    at, bo, lhs, rhs = [], [], [], []
    for h in heads:
        kk = kk0[:, sl[h]]
        kk = kk / jnp.maximum(jnp.sqrt(jnp.sum(kk * kk, axis=-1, keepdims=True)), 1e-12)
        bn_ref[0, :, sl[h]] = jnp.sum(rkr[:, sl[h]], axis=-1, keepdims=True) * v[:, sl[h]]
        b_ = kk * a[:, sl[h]]
        at.append(-kk * e_prev[:, sl[h]])
        bo.append(b_ * e_out[:, sl[h]])
        lhs.append(jnp.concatenate([at[h], rt[:, sl[h]]], axis=0))
        rhs.append(jnp.concatenate([b_ * e_neg[:, sl[h]], kt[:, sl[h]]], axis=0))
    sc = [_dot3(lhs[h], rhs[h], NT) for h in heads]
    ab = [sc[h][0:ln, 0:ln] * strict for h in heads]
    ak = [sc[h][0:ln, ln:2 * ln] * strict for h in heads]
    rb = [sc[h][ln:2 * ln, 0:ln] * incl for h in heads]
    rk = [sc[h][ln:2 * ln, ln:2 * ln] * incl for h in heads]
    tinv = [eye_l + ab[h] for h in heads]
    pw = ab
    for _ in range(int(math.log2(ln)) - 1):
        pw = [_dot3(pw[h], pw[h]) for h in heads]
        tinv = [tinv[h] + _dot3(pw[h], tinv[h]) for h in heads]
    akv = [_dot3(ak[h], v[:, sl[h]]) for h in heads]
    x = [_dot3(tinv[h], jnp.concatenate([at[h], akv[h]], axis=1)) for h in heads]
    y = [_dot3(rb[h], x[h]) for h in heads]
    rkv = [_dot3(rk[h], v[:, sl[h]]) for h in heads]
    for h in heads:
        rp_ref[0, :, sl[h]] = rt[:, sl[h]] + y[h][:, 0:n]
        yc_ref[0, :, sl[h]] = y[h][:, n:2 * n] + rkv[h]
    gm = [_dot3(x[h][:, 0:n], bo[h], TN) for h in heads]
    hm = [_dot3(jnp.concatenate([x[h][:, n:2 * n], v[:, sl[h]]], axis=0),
                jnp.concatenate([bo[h], ko[:, sl[h]]], axis=0), TN) for h in heads]
    for h in heads:
        g_ref[0, 0, :, sl[h]] = eye_n * w_last[:, sl[h]] + gm[h]
        h_ref[0, 0, :, sl[h]] = hm[h]


def _wkv_scan_kernel(rp_ref, yc_ref, bn_ref, g_ref, h_ref, s0_ref, lnw_ref, lnb_ref,
                     z_ref, sf_ref, s_ref):
    n = RW_HEAD
    d = rp_ref.shape[2]
    heads = range(d // n)
    sl = [slice(h * n, (h + 1) * n) for h in heads]

    @pl.when(pl.program_id(1) == 0)
    def _():
        s_ref[...] = s0_ref[0]

    s0 = [s_ref[:, sl[h]] for h in heads]
    y = [_dot3(rp_ref[0, :, sl[h]], s0[h], NT) + yc_ref[0, :, sl[h]] for h in heads]
    s1 = [_dot3(s0[h], g_ref[0, 0, :, sl[h]]) + h_ref[0, 0, :, sl[h]] for h in heads]
    for h in heads:
        s_ref[:, sl[h]] = s1[h]
        mean = jnp.mean(y[h], axis=-1, keepdims=True)
        yc = y[h] - mean
        var = jnp.mean(yc * yc, axis=-1, keepdims=True)
        z_ref[0, :, sl[h]] = (yc * lax.rsqrt(var + GN_EPS) * lnw_ref[:, sl[h]]
                              + lnb_ref[:, sl[h]] + bn_ref[0, :, sl[h]])
    sf_ref[0] = s_ref[...]


def _wkv(r, k, v, lw, a, s0, k_k, k_a, r_k, ln_w, ln_b, t_valid):
    nseq, t, d = r.shape
    n = RW_HEAD
    ln = min(WKV_CHUNK, t)
    nc = t // ln
    tile = pl.BlockSpec((1, ln, d), lambda s, c: (s, c, 0))
    vec = _full((1, d))
    mat = pl.BlockSpec((1, 1, n, d), lambda s, c: (s, c, 0, 0))
    big = jax.ShapeDtypeStruct((nseq, t, d), F32)
    mats = jax.ShapeDtypeStruct((nseq, nc, n, d), F32)
    rp, yc, bn, g, hm = pl.pallas_call(
        functools.partial(_wkv_prep_kernel, t_valid=None if t_valid == t else t_valid),
        grid=(nseq, nc),
        in_specs=[tile] * 5 + [vec] * 3,
        out_specs=[tile] * 3 + [mat] * 2,
        out_shape=[big] * 3 + [mats] * 2,
        compiler_params=_params(2),
        name="wkv_prep",
    )(r, k, v, lw, a, k_k.reshape(1, d), k_a.reshape(1, d), r_k.reshape(1, d))
    s0_l = jnp.transpose(s0, (0, 2, 1, 3)).reshape(nseq, n, d)
    state = pl.BlockSpec((1, n, d), lambda s, c: (s, 0, 0))
    z, sf = pl.pallas_call(
        _wkv_scan_kernel,
        grid=(nseq, nc),
        in_specs=[tile] * 3 + [mat] * 2 + [state, vec, vec],
        out_specs=[tile, state],
        out_shape=[big, jax.ShapeDtypeStruct((nseq, n, d), F32)],
        scratch_shapes=[pltpu.VMEM((n, d), F32)],
        compiler_params=_params(2),
        name="wkv_scan",
    )(rp, yc, bn, g, hm, s0_l, ln_w.reshape(1, d), ln_b.reshape(1, d))
    return z, jnp.transpose(sf.reshape(nseq, n, d // n, n), (0, 2, 1, 3))


def kernel(x_prompt, x_sample, p_prompt, p_sample, cache_k, cache_v, state_wkv, state_shift, page_table, rel_bias, norm_mix, norm_mlp, norm_ple, norm_final, att_w_qkv, att_w_o, att_lam_q1, att_lam_k1, att_lam_q2, att_lam_k2, att_subln, rw_mu, rw_w_rkv, rw_w0, rw_w1, rw_w2, rw_a0, rw_a1, rw_a2, rw_v0, rw_v1, rw_v2, rw_g1, rw_g2, rw_k_k, rw_k_a, rw_r_k, rw_ln_w, rw_ln_b, rw_w_o, mlp_w1, mlp_w2, ple_w_in, ple_w_gate):
    depth = norm_mix.shape[0]
    bp, tp, d = x_prompt.shape
    bs, ts, _ = x_sample.shape
    ts_pad = -(-ts // SUBLANES) * SUBLANES
    heads_rw = d // RW_HEAD
    page_table = page_table.astype(jnp.int32)

    xp = x_prompt.reshape(bp * tp, d)
    xs = jnp.pad(x_sample, ((0, 0), (0, ts_pad - ts), (0, 0))).reshape(bs * ts_pad, d)
    pp_all = p_prompt.reshape(depth, bp * tp, -1)
    ps_all = jnp.pad(p_sample, ((0, 0), (0, 0), (0, ts_pad - ts), (0, 0))).reshape(depth, bs * ts_pad, -1)
    flash_tabs = _flash_bias_tables(rel_bias, tp)
    decode_tabs = _decode_bias_tables(rel_bias, ts_pad, ts)

    kp_l, vp_l, ks_l, vs_l = [], [], [], []
    wkvp_l, shp_l, wkvs_l, shs_l = [], [], [], []
    vf_p = vf_s = None
    y_p = y_s = None
    for i in range(depth):
        if i % 2 == 0:
            a = i // 2
            lam_init = 0.8 - 0.6 * math.exp(-0.3 * i)
            lam = _diff_lambda(att_lam_q1[a], att_lam_k1[a], att_lam_q2[a], att_lam_k2[a], lam_init)
            wqkv = att_w_qkv[a].astype(BF16)
            wo = att_w_o[a].astype(BF16)
            qb, k, v, kb, vb = _qkv_proj(xp, norm_mix[i], wqkv)
            o = _flash_attn(qb.reshape(bp, tp, d), kb.reshape(bp, tp, d), vb.reshape(bp, tp, d),
                            lam, rel_bias, flash_tabs, att_subln[a], lam_init)
            xp = _proj_res(o.reshape(bp * tp, d), xp, wo)
            kp_l.append(k.reshape(bp, tp, ATT_HEADS, ATT_V_DIM))
            vp_l.append(v.reshape(bp, tp, ATT_HEADS, ATT_V_DIM))

            qb, k, v, _, _ = _qkv_proj(xs, norm_mix[i], wqkv)
            k = k.reshape(bs, ts_pad, d)
            v = v.reshape(bs, ts_pad, d)
            o = _decode_attn(qb.reshape(bs, ts_pad, d), k, v, cache_k, cache_v, a, page_table, lam,
                             decode_tabs, att_subln[a], lam_init)
            xs = _proj_res(o.reshape(bs * ts_pad, d), xs, wo)
            ks_l.append(k.reshape(bs, ts_pad, ATT_HEADS, ATT_V_DIM)[:, :ts])
            vs_l.append(v.reshape(bs, ts_pad, ATT_HEADS, ATT_V_DIM)[:, :ts])
        else:
            r = i // 2
            vres = None if r == 0 else (rw_v0[r - 1], rw_v1[r - 1], rw_v2[r - 1])
            wrkv = rw_w_rkv[r].astype(BF16)
            wo = rw_w_o[r].astype(BF16)
            lora = (rw_w0[r], rw_w1[r], rw_w2[r], rw_a0[r], rw_a1[r], rw_a2[r], rw_g1[r], rw_g2[r])
            heads = (rw_k_k[r], rw_k_a[r], rw_r_k[r].reshape(d), rw_ln_w[r], rw_ln_b[r])

            def time_mix(x2, nseq, t, shift, s0, v_first, t_valid):
                rr, kk, vv, lw, aa, gg, last = _rwkv_pre(
                    x2.reshape(nseq, t, d), shift, norm_mix[i], rw_mu[r], wrkv, *lora, vres,
                    v_first, t_valid)
                z, s_fin = _wkv(rr, kk, vv, lw, aa, s0, *heads, t_valid)
                out = _proj_res(z.reshape(nseq * t, d), x2, wo, gate2=gg.reshape(nseq * t, d))
                return out, s_fin, last.reshape(nseq, d), (vv if vres is None else v_first)

            xp, s_p, last_p, vf_p = time_mix(
                xp, bp, tp, jnp.zeros((bp, d), F32),
                jnp.zeros((bp, heads_rw, RW_HEAD, RW_HEAD), F32), vf_p, tp)
            xs, s_s, last_s, vf_s = time_mix(xs, bs, ts_pad, state_shift[r], state_wkv[r], vf_s, ts)
            wkvp_l.append(s_p)
            shp_l.append(last_p)
            wkvs_l.append(s_s)
            shs_l.append(last_s)
        final = i == depth - 1
        ffn_w = (norm_mlp[i], mlp_w1[i].astype(BF16), mlp_w2[i].astype(BF16), norm_ple[i],
                 ple_w_in[i].astype(BF16), ple_w_gate[i].astype(BF16), norm_final)
        xp, y_p = _ffn_ple(xp, pp_all[i], *ffn_w, final)
        xs, y_s = _ffn_ple(xs, ps_all[i], *ffn_w, final)

    y_prompt = y_p.reshape(bp, tp, d)
    y_sample = y_s.reshape(bs, ts_pad, d)[:, :ts]
    return (y_prompt, y_sample, jnp.stack(kp_l), jnp.stack(vp_l), jnp.stack(wkvp_l),
            jnp.stack(shp_l), jnp.stack(ks_l), jnp.stack(vs_l), jnp.stack(wkvs_l),
            jnp.stack(shs_l))
```

```python
import functools
import math

import numpy as np
import jax
import jax.numpy as jnp
from jax import lax
from jax.experimental import pallas as pl
from jax.experimental.pallas import tpu as pltpu

F32 = jnp.float32
BF16 = jnp.bfloat16
HIGHEST = lax.Precision.HIGHEST

ATT_HEADS = 8
ATT_HEAD_DIM = 64
ATT_V_DIM = 128
N_BUCKETS = 32
MAX_EXACT = 16
MAX_DISTANCE = 128
PAGE_SIZE = 128
RW_HEAD = 64
NORM_EPS = 1e-6
SUBLN_EPS = 1e-5
GN_EPS = 64e-5
NEG_INF = -1e30
LOG2E = math.log2(math.e)

LANES = 128
SUBLANES = 8
VMEM_LIMIT = 56 * 1024 * 1024
TOKEN_TILE = 512
RWKV_TILE = 256
ATT_BLOCK = 512
ATT_FAR_UNROLL = 4
ATT_ROW_CHUNK = 128
DECODE_PAGES = 4
WKV_CHUNK = 64
FF_CHUNK = 1024

NN = (((1,), (0,)), ((), ()))
NT = (((1,), (1,)), ((), ()))
TN = (((0,), (0,)), ((), ()))


def _params(n_axes):
    return pltpu.CompilerParams(dimension_semantics=("arbitrary",) * n_axes,
                                vmem_limit_bytes=VMEM_LIMIT)


def _rms(x, g, eps):
    return x * lax.rsqrt(jnp.mean(x * x, axis=-1, keepdims=True) + eps) * g


def _sigmoid(x):
    return 1.0 / (1.0 + jnp.exp(-x))


def _bdot(a, b):
    return jnp.dot(a, b, preferred_element_type=F32)


def _split(a):
    hi = a.astype(BF16)
    return hi, (a - hi.astype(F32)).astype(BF16)


def _dot3(a, b, dims=NN):
    ah, al = _split(a)
    bh, bl = _split(b)
    d = lambda x, y: lax.dot_general(x, y, dims, preferred_element_type=F32)
    return d(ah, bh) + (d(ah, bl) + d(al, bh))


def _full(shape):
    zeros = (0,) * len(shape)
    return pl.BlockSpec(shape, lambda *_: zeros)


def _qkv_kernel(x_ref, g_ref, w_ref, qb_ref, k_ref, v_ref, kb_ref, vb_ref):
    d = x_ref.shape[1]
    xn = _rms(x_ref[...], g_ref[...], NORM_EPS).astype(BF16)
    q = _bdot(xn, w_ref[:, 0:d])
    qb_ref[...] = (q * (ATT_HEAD_DIM ** -0.5 * LOG2E)).astype(BF16)
    k = _bdot(xn, w_ref[:, d:2 * d])
    k_ref[...] = k
    kb_ref[...] = k.astype(BF16)
    v = _bdot(xn, w_ref[:, 2 * d:3 * d])
    v_ref[...] = v
    vb_ref[...] = v.astype(BF16)


def _qkv_proj(x2, g, w_bf):
    m, d = x2.shape
    tm = min(TOKEN_TILE, m)
    row = pl.BlockSpec((tm, d), lambda i: (i, 0))
    return pl.pallas_call(
        _qkv_kernel,
        grid=(m // tm,),
        in_specs=[row, _full((1, d)), _full((d, 3 * d))],
        out_specs=[row, row, row, row, row],
        out_shape=[jax.ShapeDtypeStruct((m, d), BF16),
                   jax.ShapeDtypeStruct((m, d), F32),
                   jax.ShapeDtypeStruct((m, d), F32),
                   jax.ShapeDtypeStruct((m, d), BF16),
                   jax.ShapeDtypeStruct((m, d), BF16)],
        compiler_params=_params(1),
        name="qkv_proj",
    )(x2, g.reshape(1, d), w_bf)


def _proj_res_kernel(a_ref, res_ref, w_ref, o_ref):
    o_ref[...] = res_ref[...] + _bdot(a_ref[...].astype(BF16), w_ref[...])


def _proj_gate_res_kernel(a_ref, gate_ref, res_ref, w_ref, o_ref):
    a = (a_ref[...] * gate_ref[...]).astype(BF16)
    o_ref[...] = res_ref[...] + _bdot(a, w_ref[...])


def _proj_res(a2, res2, w_bf, gate2=None):
    m, d = res2.shape
    tm = min(TOKEN_TILE, m)
    row = pl.BlockSpec((tm, d), lambda i: (i, 0))
    if gate2 is None:
        kern, ins, specs = _proj_res_kernel, (a2, res2, w_bf), [row, row, _full((d, d))]
    else:
        kern, ins = _proj_gate_res_kernel, (a2, gate2, res2, w_bf)
        specs = [row, row, row, _full((d, d))]
    return pl.pallas_call(
        kern,
        grid=(m // tm,),
        in_specs=specs,
        out_specs=row,
        out_shape=jax.ShapeDtypeStruct((m, d), F32),
        compiler_params=_params(1),
        name="proj_res",
    )(*ins)


def _ffn_kernel(x_ref, p_ref, gm_ref, w1_ref, w2_ref, gp_ref, win_ref, wg_ref, gf_ref,
                o_ref, *y_ref):
    x = x_ref[...]
    h = _rms(x, gm_ref[...], NORM_EPS).astype(BF16)
    acc = x
    d_ff = w1_ref.shape[1]
    for c in range(d_ff // FF_CHUNK):
        cols = slice(c * FF_CHUNK, (c + 1) * FF_CHUNK)
        a = jnp.maximum(_bdot(h, w1_ref[:, cols]), 0.0)
        acc = acc + _bdot((a * a).astype(BF16), w2_ref[cols, :])
    gate = _sigmoid(_bdot(_rms(acc, gp_ref[...], NORM_EPS).astype(BF16), wg_ref[...]))
    out = acc + _bdot(p_ref[...].astype(BF16), win_ref[...]) * gate
    o_ref[...] = out
    if y_ref:
        y_ref[0][...] = _rms(out, gf_ref[...], NORM_EPS)


def _ffn_ple(x2, p2, g_mlp, w1_bf, w2_bf, g_ple, win_bf, wg_bf, g_final, emit_final):
    m, d = x2.shape
    pd = p2.shape[1]
    d_ff = w1_bf.shape[1]
    tm = min(TOKEN_TILE, m)
    row = pl.BlockSpec((tm, d), lambda i: (i, 0))
    once = dict(pipeline_mode=pl.Buffered(1))
    specs = [row, pl.BlockSpec((tm, pd), lambda i: (i, 0)), _full((1, d)),
             pl.BlockSpec((d, d_ff), lambda i: (0, 0), **once),
             pl.BlockSpec((d_ff, d), lambda i: (0, 0), **once),
             _full((1, d)),
             pl.BlockSpec((pd, d), lambda i: (0, 0), **once),
             pl.BlockSpec((d, d), lambda i: (0, 0), **once),
             _full((1, d))]
    n_out = 2 if emit_final else 1
    out = pl.pallas_call(
        _ffn_kernel,
        grid=(m // tm,),
        in_specs=specs,
        out_specs=[row] * n_out,
        out_shape=[jax.ShapeDtypeStruct((m, d), F32)] * n_out,
        compiler_params=_params(1),
        name="ffn_ple",
    )(x2, p2, g_mlp.reshape(1, d), w1_bf, w2_bf, g_ple.reshape(1, d), win_bf, wg_bf,
      g_final.reshape(1, d))
    return out if emit_final else (out[0], None)


def _bucket_np(n):
    n = np.asarray(n, np.int64)
    nf = np.maximum(n, 1).astype(np.float32)
    large = MAX_EXACT + (np.log(nf / np.float32(MAX_EXACT))
                         / np.float32(math.log(MAX_DISTANCE / MAX_EXACT))
                         * np.float32(N_BUCKETS - MAX_EXACT)).astype(np.int32)
    large = np.minimum(large, N_BUCKETS - 1)
    return np.where(n < MAX_EXACT, n, large).astype(np.int32)


def _bias_kernel(relb_ref, idx_ref, o_ref):
    h = pl.program_id(0)
    idx = idx_ref[...]
    acc = jnp.full(idx.shape, NEG_INF, F32)
    far = relb_ref[N_BUCKETS - 1, h]
    for b in range(N_BUCKETS):
        acc = jnp.where(idx == b, (relb_ref[b, h] - far) * LOG2E, acc)
    o_ref[0] = acc


def _bias_tables(rel_bias, idx):
    r, c = idx.shape
    return pl.pallas_call(
        _bias_kernel,
        grid=(ATT_HEADS,),
        in_specs=[pl.BlockSpec(memory_space=pltpu.SMEM), _full((r, c))],
        out_specs=pl.BlockSpec((1, r, c), lambda h: (h, 0, 0)),
        out_shape=jax.ShapeDtypeStruct((ATT_HEADS, r, c), F32),
        compiler_params=_params(1),
        name="bias_tables",
    )(rel_bias, jnp.asarray(idx.astype(np.int32)))


def _lam_kernel(q1_ref, k1_ref, q2_ref, k2_ref, o_ref, *, lam_init):
    e1 = jnp.exp(jnp.sum(q1_ref[...] * k1_ref[...], axis=-1, keepdims=True))
    e2 = jnp.exp(jnp.sum(q2_ref[...] * k2_ref[...], axis=-1, keepdims=True))
    o_ref[...] = jnp.broadcast_to(e1 - e2 + lam_init, o_ref.shape)


def _diff_lambda(lq1, lk1, lq2, lk2, lam_init):
    n = lq1.shape[0]
    out = pl.pallas_call(
        functools.partial(_lam_kernel, lam_init=lam_init),
        in_specs=[_full((1, n))] * 4,
        out_specs=_full((1, LANES)),
        out_shape=jax.ShapeDtypeStruct((1, LANES), F32),
        grid=(1,),
        compiler_params=_params(1),
        name="diff_lambda",
    )(lq1.reshape(1, n), lk1.reshape(1, n), lq2.reshape(1, n), lk2.reshape(1, n))
    return out[0, :1]


def _online_softmax(s, v1, m_ref, acc_ref, rows):
    tiles = [s[:, c * LANES:(c + 1) * LANES] for c in range(s.shape[1] // LANES)]
    m_prev = m_ref[rows, :]
    row_max = jnp.max(functools.reduce(jnp.maximum, tiles), axis=1, keepdims=True)
    m_new = jnp.maximum(m_prev, row_max)
    alpha = jnp.exp2(m_prev - m_new)
    p = jnp.concatenate([jnp.exp2(t - m_new).astype(BF16) for t in tiles], axis=1)
    acc_ref[rows, :] = jnp.concatenate([alpha, alpha], axis=1) * acc_ref[rows, :] + _bdot(p, v1)
    m_ref[rows, :] = m_new


def _flash_kernel(lam_ref, q_ref, k_ref, v_ref, b0_ref, b1_ref, g_ref, o_ref,
                  qs_ref, v1_ref, m_ref, acc_ref, *, blk, chunk, lam_init):
    i = pl.program_id(2)

    @pl.when(i == 0)
    def _():
        v1_ref[:, 0:ATT_V_DIM] = v_ref[0]
        v1_ref[:, ATT_V_DIM:2 * ATT_V_DIM] = jnp.ones((v1_ref.shape[0], ATT_V_DIM), BF16)

    q = q_ref[0]
    lane = lax.broadcasted_iota(jnp.int32, q.shape, 1)
    zero = jnp.zeros_like(q)
    qs_ref[0:blk, :] = jnp.where(lane < ATT_HEAD_DIM, q, zero)
    qs_ref[blk:2 * blk, :] = jnp.where(lane >= ATT_HEAD_DIM, q, zero)
    m_ref[...] = jnp.full(m_ref.shape, NEG_INF, F32)
    acc_ref[...] = jnp.zeros(acc_ref.shape, F32)

    def step(j, bias):
        cols = pl.ds(pl.multiple_of(j * blk, blk), blk)
        kb = k_ref[0, cols, :]
        v1 = v1_ref[cols, :]
        for c in range(2 * blk // chunk):
            rows = slice(c * chunk, (c + 1) * chunk)
            s = lax.dot_general(qs_ref[rows, :], kb, NT, preferred_element_type=F32)
            if bias is not None:
                r0 = (c * chunk) % blk
                s = s + bias[r0:r0 + chunk, :]
            _online_softmax(s, v1, m_ref, acc_ref, rows)

    n_far = jnp.maximum(i - 1, 0)

    def far_body(jj, carry):
        for u in range(ATT_FAR_UNROLL):
            step(jj * ATT_FAR_UNROLL + u, None)
        return carry

    lax.fori_loop(0, n_far // ATT_FAR_UNROLL, far_body, 0)
    done = (n_far // ATT_FAR_UNROLL) * ATT_FAR_UNROLL
    bit = ATT_FAR_UNROLL // 2
    while bit:
        take = (n_far - done) & bit

        @pl.when(take != 0)
        def _(done=done, bit=bit):
            for u in range(bit):
                step(done + u, None)

        done = done + take
        bit //= 2

    @pl.when(i >= 1)
    def _():
        step(i - 1, b1_ref[0])
        step(i, b0_ref[0])

    @pl.when(i == 0)
    def _():
        step(0, b0_ref[0])

    acc = acc_ref[...]
    o12 = acc[:, 0:ATT_V_DIM] / acc[:, ATT_V_DIM:2 * ATT_V_DIM]
    o = o12[0:blk, :] - lam_ref[0] * o12[blk:2 * blk, :]
    y = _rms(o, g_ref[...], SUBLN_EPS) * (1.0 - lam_init)
    o_ref[0] = y.astype(BF16)


def _flash_bias_tables(rel_bias, t):
    blk = min(ATT_BLOCK, t)
    assert blk >= MAX_DISTANCE and t % blk == 0
    r = np.arange(blk)[:, None]
    c = np.arange(blk)[None, :]
    idx_diag = np.where(r >= c, _bucket_np(np.maximum(r - c, 0)), -1)
    idx_sub = _bucket_np(blk + r - c)
    assert (_bucket_np(np.arange(blk + 1, 4 * blk)) == N_BUCKETS - 1).all()
    return _bias_tables(rel_bias, np.concatenate([idx_diag, idx_sub], axis=0))


def _flash_attn(qb, kb, vb, lam, tabs, subln_g, lam_init):
    b, t, d = qb.shape
    blk = min(ATT_BLOCK, t)
    qspec = pl.BlockSpec((1, blk, ATT_V_DIM), lambda bi, h, i: (bi, i, h))
    kvspec = pl.BlockSpec((1, t, ATT_V_DIM), lambda bi, h, i: (bi, 0, h))
    return pl.pallas_call(
        functools.partial(_flash_kernel, blk=blk, chunk=min(ATT_ROW_CHUNK, blk),
                          lam_init=lam_init),
        grid=(b, ATT_HEADS, t // blk),
        in_specs=[pl.BlockSpec(memory_space=pltpu.SMEM),
                  qspec, kvspec, kvspec,
                  pl.BlockSpec((1, blk, blk), lambda bi, h, i: (h, 0, 0)),
                  pl.BlockSpec((1, blk, blk), lambda bi, h, i: (h, 1, 0)),
                  _full((1, ATT_V_DIM))],
        out_specs=qspec,
        out_shape=jax.ShapeDtypeStruct((b, t, d), BF16),
        scratch_shapes=[pltpu.VMEM((2 * blk, ATT_V_DIM), BF16),
                        pltpu.VMEM((t, 2 * ATT_V_DIM), BF16),
                        pltpu.VMEM((2 * blk, LANES), F32),
                        pltpu.VMEM((2 * blk, 2 * ATT_V_DIM), F32)],
        compiler_params=_params(3),
        name="flash_diff_attn",
    )(lam, qb, kb, vb, tabs, tabs, subln_g.reshape(1, ATT_V_DIM))


def _decode_kernel(pt_ref, lam_ref, q_ref, *refs, tq, lam_init):
    del pt_ref
    ck_refs = refs[0:DECODE_PAGES]
    cv_refs = refs[DECODE_PAGES:2 * DECODE_PAGES]
    (kn_ref, vn_ref, tab_ref, tabn_ref, g_ref, o_ref,
     qa_ref, m_ref, l_ref, acc_ref) = refs[2 * DECODE_PAGES:]
    p = pl.program_id(1)
    n_groups = pl.num_programs(1) - 1
    grp = 2 * tq

    @pl.when(p == 0)
    def _():
        for h in range(ATT_HEADS):
            qh = q_ref[0, :, ATT_V_DIM * h:ATT_V_DIM * (h + 1)].astype(F32)
            lane = lax.broadcasted_iota(jnp.int32, qh.shape, 1)
            qa_ref[h * grp:h * grp + tq, :] = jnp.where(lane < ATT_HEAD_DIM, qh, 0.0)
            qa_ref[h * grp + tq:(h + 1) * grp, :] = jnp.where(lane >= ATT_HEAD_DIM, qh, 0.0)
        m_ref[...] = jnp.full(m_ref.shape, NEG_INF, F32)
        l_ref[...] = jnp.zeros(l_ref.shape, F32)
        acc_ref[...] = jnp.zeros(acc_ref.shape, F32)

    def step(pages):
        qa = qa_ref[...].astype(BF16)
        tiles = []
        for k3, _, tab in pages:
            k2 = k3.reshape(-1, ATT_V_DIM).astype(BF16)
            s = lax.dot_general(qa, k2, NT, preferred_element_type=F32) + tab
            tiles += [s[:, c * LANES:(c + 1) * LANES] for c in range(-(-s.shape[1] // LANES))]
        m_prev = m_ref[...]
        row_max = jnp.max(functools.reduce(jnp.maximum, tiles), axis=1, keepdims=True)
        m_new = jnp.maximum(m_prev, row_max)
        alpha = jnp.exp2(m_prev - m_new)
        ps = [jnp.exp2(t - m_new[:, 0:t.shape[1]]) for t in tiles]
        row_sum = jnp.sum(functools.reduce(jnp.add, ps), axis=1, keepdims=True)
        l_ref[...] = alpha * l_ref[...] + row_sum
        acc = alpha * acc_ref[...]
        per_page = len(ps) // len(pages)
        for n, (_, v3, _) in enumerate(pages):
            v2 = v3.reshape(-1, ATT_V_DIM).astype(BF16)
            pb = jnp.concatenate([x.astype(BF16) for x in ps[n * per_page:(n + 1) * per_page]],
                                 axis=1)
            acc = acc + _bdot(pb, v2)
        acc_ref[...] = acc
        m_ref[...] = m_new

    def cached(last_kind):
        kinds = [0] * (DECODE_PAGES - 1) + [last_kind]
        return [(ck[...], cv[...], tab_ref[kind]) for ck, cv, kind in zip(ck_refs, cv_refs, kinds)]

    @pl.when(p < n_groups - 1)
    def _():
        step(cached(0))

    @pl.when(p == n_groups - 1)
    def _():
        step(cached(1))

    @pl.when(p == n_groups)
    def _():
        step([(kn_ref[0], vn_ref[0], tabn_ref[...])])
        o12 = acc_ref[...] / l_ref[...]
        for h in range(ATT_HEADS):
            o = o12[h * grp:h * grp + tq, :] - lam_ref[0] * o12[h * grp + tq:(h + 1) * grp, :]
            y = _rms(o, g_ref[...], SUBLN_EPS) * (1.0 - lam_init)
            o_ref[0, :, ATT_V_DIM * h:ATT_V_DIM * (h + 1)] = y


def _decode_bias_tables(rel_bias, tq, t_valid):
    tt = np.arange(tq)[:, None]
    cc = np.arange(PAGE_SIZE)[None, :]
    idx_far = np.full((tq, PAGE_SIZE), N_BUCKETS - 1)
    idx_last = _bucket_np(PAGE_SIZE - cc + tt)
    assert (_bucket_np(PAGE_SIZE + 1 + 0 * cc + tt) == N_BUCKETS - 1).all()
    idx_new = np.where((cc <= tt) & (cc < t_valid), _bucket_np(np.maximum(tt - cc, 0)), -1)
    tabs = _bias_tables(rel_bias, np.concatenate([idx_far, idx_last, idx_new], axis=0))
    h = ATT_HEADS
    tb = jnp.transpose(tabs.reshape(h, 3, tq, PAGE_SIZE), (1, 0, 2, 3))
    tb = jnp.broadcast_to(tb[:, :, None, :, :, None], (3, h, 2, tq, PAGE_SIZE, h))
    own = np.eye(h, dtype=bool)[None, :, None, None, None, :]
    tb = jnp.where(own, tb, NEG_INF)
    cached = tb[0:2].reshape(2, h * 2 * tq, PAGE_SIZE * h)
    new = tb[2, :, :, :, 0:tq, :].reshape(h * 2 * tq, tq * h)
    return cached, new


def _decode_attn(qb, k, v, cache_k, cache_v, layer, page_table, lam, tabs, subln_g, lam_init):
    bs, tq, d = qb.shape
    n_pages = page_table.shape[1]
    heads = (ATT_HEADS, ATT_V_DIM)
    kn = k.reshape(bs, tq, *heads)
    vn = v.reshape(bs, tq, *heads)
    tabs, tabs_new = tabs
    rows = ATT_HEADS * 2 * tq
    assert n_pages % DECODE_PAGES == 0
    n_groups = n_pages // DECODE_PAGES

    def page_map(which):
        def index(bi, p, pt):
            return (layer, pt[bi, DECODE_PAGES * jnp.minimum(p, n_groups - 1) + which], 0, 0, 0)
        return index

    seq3 = lambda bi, p, pt: (bi, 0, 0)
    seq4 = lambda bi, p, pt: (bi, 0, 0, 0)
    page = lambda which: pl.BlockSpec((None, None, PAGE_SIZE) + heads, page_map(which))
    new = pl.BlockSpec((1, tq) + heads, seq4)
    grid_spec = pltpu.PrefetchScalarGridSpec(
        num_scalar_prefetch=1,
        grid=(bs, n_groups + 1),
        in_specs=[pl.BlockSpec(memory_space=pltpu.SMEM),
                  pl.BlockSpec((1, tq, d), seq3),
                  *[page(n) for n in range(DECODE_PAGES)] * 2, new, new,
                  pl.BlockSpec(tabs.shape, lambda bi, p, pt: (0, 0, 0)),
                  pl.BlockSpec(tabs_new.shape, lambda bi, p, pt: (0, 0)),
                  pl.BlockSpec((1, ATT_V_DIM), lambda bi, p, pt: (0, 0))],
        out_specs=pl.BlockSpec((1, tq, d), seq3),
        scratch_shapes=[pltpu.VMEM((rows, ATT_V_DIM), F32),
                        pltpu.VMEM((rows, LANES), F32),
                        pltpu.VMEM((rows, LANES), F32),
                        pltpu.VMEM((rows, ATT_V_DIM), F32)])
    return pl.pallas_call(
        functools.partial(_decode_kernel, tq=tq, lam_init=lam_init),
        grid_spec=grid_spec,
        out_shape=jax.ShapeDtypeStruct((bs, tq, d), F32),
        compiler_params=_params(2),
        name="decode_diff_attn",
    )(page_table, lam, qb, *[cache_k] * DECODE_PAGES, *[cache_v] * DECODE_PAGES, kn, vn, tabs,
      tabs_new, subln_g.reshape(1, ATT_V_DIM))


def _rwkv_pre_kernel(*refs, has_vres, t_last):
    if has_vres:
        (x_ref, sh_ref, gm_ref, mu_ref, wrkv_ref, w0_ref, w1_ref, w2_ref, a0_ref, a1_ref, a2_ref,
         g1_ref, g2_ref, vf_ref, v0_ref, v1_ref, v2_ref,
         r_ref, k_ref, v_ref, lw_ref, a_ref, g_ref, last_ref, carry_ref) = refs
    else:
        (x_ref, sh_ref, gm_ref, mu_ref, wrkv_ref, w0_ref, w1_ref, w2_ref, a0_ref, a1_ref, a2_ref,
         g1_ref, g2_ref,
         r_ref, k_ref, v_ref, lw_ref, a_ref, g_ref, last_ref, carry_ref) = refs
    ti = pl.program_id(1)
    tm = x_ref.shape[1]
    xn = _rms(x_ref[0], gm_ref[...], NORM_EPS)

    @pl.when(ti == 0)
    def _():
        carry_ref[...] = sh_ref[0]

    rolled = pltpu.roll(xn, 1, axis=0)
    rid = lax.broadcasted_iota(jnp.int32, xn.shape, 0)
    x_prev = jnp.where(rid == 0, carry_ref[...], rolled)
    carry_ref[...] = xn[tm - 1:tm, :]

    @pl.when(ti == pl.num_programs(1) - 1)
    def _():
        last_ref[0] = xn[t_last:t_last + 1, :]

    xx = x_prev - xn
    mix = lambda n: (xn + xx * mu_ref[n:n + 1, :]).astype(BF16)
    xr, xw, xk, xv, xa, xg = [mix(n) for n in range(6)]
    r_ref[0] = _bdot(xr, wrkv_ref[0])
    k_ref[0] = _bdot(xk, wrkv_ref[1])
    v = _bdot(xv, wrkv_ref[2])
    if has_vres:
        lo = _bdot(_bdot(xv, v1_ref[...]).astype(BF16), v2_ref[...])
        v = v + (vf_ref[0] - v) * _sigmoid(v0_ref[...] + lo)
    v_ref[0] = v
    wl = w0_ref[...] + _bdot(jnp.tanh(_bdot(xw, w1_ref[...])).astype(BF16), w2_ref[...])
    z = -wl
    softplus = jnp.maximum(z, 0.0) + jnp.log(1.0 + jnp.exp(-jnp.abs(z)))
    lw_ref[0] = -jnp.exp(-softplus - 0.5)
    a_ref[0] = _sigmoid(a0_ref[...] + _bdot(_bdot(xa, a1_ref[...]).astype(BF16), a2_ref[...]))
    g_ref[0] = _bdot(_sigmoid(_bdot(xg, g1_ref[...])).astype(BF16), g2_ref[...])


def _pad_lora(w_in, w_out):
    n = w_in.shape[1]
    n_pad = -(-n // LANES) * LANES - n
    return (jnp.pad(w_in, ((0, 0), (0, n_pad))).astype(BF16),
            jnp.pad(w_out, ((0, n_pad), (0, 0))).astype(BF16))


def _rwkv_pre(x3, shift, g_mix, mu, wrkv_bf, w0, w1, w2, a0, a1, a2, g1, g2, vres, v_first, t_valid):
    nseq, t, d = x3.shape
    tm = min(RWKV_TILE, t)
    nt = t // tm
    t_last = (t_valid - 1) - (nt - 1) * tm
    tile = pl.BlockSpec((1, tm, d), lambda s, i: (s, i, 0))
    vec = _full((1, d))
    w1p, w2p = _pad_lora(w1, w2)
    a1p, a2p = _pad_lora(a1, a2)
    g1p, g2p = _pad_lora(g1, g2)
    ins = [x3, shift.reshape(nseq, 1, d), g_mix.reshape(1, d), mu, wrkv_bf,
           w0.reshape(1, d), w1p, w2p, a0.reshape(1, d), a1p, a2p, g1p, g2p]
    specs = [tile, pl.BlockSpec((1, 1, d), lambda s, i: (s, 0, 0)), vec, _full(mu.shape),
             pl.BlockSpec(wrkv_bf.shape, lambda s, i: (0, 0, 0), pipeline_mode=pl.Buffered(1)),
             vec, _full(w1p.shape), _full(w2p.shape),
             vec, _full(a1p.shape), _full(a2p.shape), _full(g1p.shape), _full(g2p.shape)]
    if vres is not None:
        v0, v1, v2 = vres
        v1p, v2p = _pad_lora(v1, v2)
        ins += [v_first, v0.reshape(1, d), v1p, v2p]
        specs += [tile, vec, _full(v1p.shape), _full(v2p.shape)]
    big = jax.ShapeDtypeStruct((nseq, t, d), F32)
    outs = pl.pallas_call(
        functools.partial(_rwkv_pre_kernel, has_vres=vres is not None, t_last=t_last),
        grid=(nseq, nt),
        in_specs=specs,
        out_specs=[tile] * 6 + [pl.BlockSpec((1, 1, d), lambda s, i: (s, 0, 0))],
        out_shape=[big] * 6 + [jax.ShapeDtypeStruct((nseq, 1, d), F32)],
        scratch_shapes=[pltpu.VMEM((1, d), F32)],
        compiler_params=_params(2),
        name="rwkv_pre",
    )(*ins)
    return outs


def _wkv_prep_kernel(r_ref, k_ref, v_ref, lw_ref, a_ref, kk_ref, ka_ref, rk_ref,
                     rp_ref, yc_ref, bn_ref, g_ref, h_ref, *, t_valid):
    ln = r_ref.shape[1]
    d = r_ref.shape[2]
    n = RW_HEAD
    heads = range(d // n)
    sl = [slice(h * n, (h + 1) * n) for h in heads]
    ri = lax.broadcasted_iota(jnp.int32, (ln, ln), 0)
    ci = lax.broadcasted_iota(jnp.int32, (ln, ln), 1)
    incl = (ri >= ci).astype(F32)
    strict = (ri > ci).astype(F32)
    eye_l = (ri == ci).astype(F32)
    eye_n = (lax.broadcasted_iota(jnp.int32, (n, n), 0)
             == lax.broadcasted_iota(jnp.int32, (n, n), 1)).astype(F32)
    r, k, v, lw, a = r_ref[0], k_ref[0], v_ref[0], lw_ref[0], a_ref[0]
    if t_valid is not None:
        t_glob = pl.program_id(1) * ln + lax.broadcasted_iota(jnp.int32, (ln, 1), 0)
        live = (t_glob < t_valid).astype(F32)
        r, k, v, lw, a = r * live, k * live, v * live, lw * live, a * live

    cum = jnp.dot(incl, lw, precision=HIGHEST, preferred_element_type=F32)
    cum_last = cum[ln - 1:ln, :]
    e_neg = jnp.exp(-cum)
    e_out = jnp.exp(cum_last - cum)
    e_prev = jnp.exp(cum - lw)
    w_last = jnp.exp(cum_last)
    rt = r * jnp.exp(cum)
    kmod = k * (1.0 + (a - 1.0) * ka_ref[...])
    kt = kmod * e_neg
    ko = kmod * e_out
    kk0 = k * kk_ref[...]
    rkr = r * kmod * rk_ref[...]

    bn_ref[0] = jnp.concatenate(
        [jnp.broadcast_to(jnp.sum(rkr[:, sl[h]], axis=-1, keepdims=True), (ln, n)) for h in heads],
        axis=1) * v
    at, bo, lhs, rhs = [], [], [], []
    for h in heads:
        kk = kk0[:, sl[h]]
        kk = kk / jnp.maximum(jnp.sqrt(jnp.sum(kk * kk, axis=-1, keepdims=True)), 1e-12)
        b_ = kk * a[:, sl[h]]
        at.append(-kk * e_prev[:, sl[h]])
        bo.append(b_ * e_out[:, sl[h]])
        lhs.append(jnp.concatenate([at[h], rt[:, sl[h]]], axis=0))
        rhs.append(jnp.concatenate([b_ * e_neg[:, sl[h]], kt[:, sl[h]]], axis=0))
    sc = [_dot3(lhs[h], rhs[h], NT) for h in heads]
    ab = [sc[h][0:ln, 0:ln] * strict for h in heads]
    ak = [sc[h][0:ln, ln:2 * ln] * strict for h in heads]
    rb = [sc[h][ln:2 * ln, 0:ln] * incl for h in heads]
    rk = [sc[h][ln:2 * ln, ln:2 * ln] * incl for h in heads]
    tinv = [eye_l + ab[h] for h in heads]
    pw = ab
    for _ in range(int(math.log2(ln)) - 1):
        pw = [_dot3(pw[h], pw[h]) for h in heads]
        tinv = [tinv[h] + _dot3(pw[h], tinv[h]) for h in heads]
    akv = [_dot3(ak[h], v[:, sl[h]]) for h in heads]
    x = [_dot3(tinv[h], jnp.concatenate([at[h], akv[h]], axis=1)) for h in heads]
    y = [_dot3(rb[h], x[h]) for h in heads]
    rkv = [_dot3(rk[h], v[:, sl[h]]) for h in heads]
    rp_ref[0] = rt + jnp.concatenate([y[h][:, 0:n] for h in heads], axis=1)
    yc_ref[0] = jnp.concatenate([y[h][:, n:2 * n] + rkv[h] for h in heads], axis=1)
    gm = [_dot3(bo[h], x[h][:, 0:n], TN) for h in heads]
    hm = [_dot3(jnp.concatenate([bo[h], ko[:, sl[h]]], axis=0),
                jnp.concatenate([x[h][:, n:2 * n], v[:, sl[h]]], axis=0), TN) for h in heads]
    g_ref[0, 0] = jnp.concatenate([eye_n] * len(heads), axis=1) * w_last + jnp.concatenate(gm, axis=1)
    h_ref[0, 0] = jnp.concatenate(hm, axis=1)


def _wkv_scan_kernel(rp_ref, yc_ref, bn_ref, g_ref, h_ref, s0_ref, lnw_ref, lnb_ref,
                     z_ref, sf_ref, s_ref):
    n = RW_HEAD
    d = rp_ref.shape[2]
    heads = range(d // n)
    sl = [slice(h * n, (h + 1) * n) for h in heads]

    @pl.when(pl.program_id(1) == 0)
    def _():
        s_ref[...] = s0_ref[0]

    s0 = [s_ref[:, sl[h]] for h in heads]
    y = [_dot3(rp_ref[0, :, sl[h]], s0[h]) + yc_ref[0, :, sl[h]] for h in heads]
    s1 = [_dot3(g_ref[0, 0, :, sl[h]], s0[h]) + h_ref[0, 0, :, sl[h]] for h in heads]
    s_new = jnp.concatenate(s1, axis=1)
    s_ref[...] = s_new
    sf_ref[0] = s_new
    yc = [y[h] - jnp.mean(y[h], axis=-1, keepdims=True) for h in heads]
    var = [jnp.mean(yc[h] * yc[h], axis=-1, keepdims=True) for h in heads]
    yn = jnp.concatenate([yc[h] * lax.rsqrt(var[h] + GN_EPS) for h in heads], axis=1)
    z_ref[0] = yn * lnw_ref[...] + lnb_ref[...] + bn_ref[0]


def _wkv(r, k, v, lw, a, s0, k_k, k_a, r_k, ln_w, ln_b, t_valid):
    nseq, t, d = r.shape
    n = RW_HEAD
    ln = min(WKV_CHUNK, t)
    nc = t // ln
    tile = pl.BlockSpec((1, ln, d), lambda s, c: (s, c, 0))
    vec = _full((1, d))
    mat = pl.BlockSpec((1, 1, n, d), lambda s, c: (s, c, 0, 0))
    big = jax.ShapeDtypeStruct((nseq, t, d), F32)
    mats = jax.ShapeDtypeStruct((nseq, nc, n, d), F32)
    rp, yc, bn, g, hm = pl.pallas_call(
        functools.partial(_wkv_prep_kernel, t_valid=None if t_valid == t else t_valid),
        grid=(nseq, nc),
        in_specs=[tile] * 5 + [vec] * 3,
        out_specs=[tile] * 3 + [mat] * 2,
        out_shape=[big] * 3 + [mats] * 2,
        compiler_params=_params(2),
        name="wkv_prep",
    )(r, k, v, lw, a, k_k.reshape(1, d), k_a.reshape(1, d), r_k.reshape(1, d))
    s0_l = jnp.transpose(s0, (0, 3, 1, 2)).reshape(nseq, n, d)
    state = pl.BlockSpec((1, n, d), lambda s, c: (s, 0, 0))
    z, sf = pl.pallas_call(
        _wkv_scan_kernel,
        grid=(nseq, nc),
        in_specs=[tile] * 3 + [mat] * 2 + [state, vec, vec],
        out_specs=[tile, state],
        out_shape=[big, jax.ShapeDtypeStruct((nseq, n, d), F32)],
        scratch_shapes=[pltpu.VMEM((n, d), F32)],
        compiler_params=_params(2),
        name="wkv_scan",
    )(rp, yc, bn, g, hm, s0_l, ln_w.reshape(1, d), ln_b.reshape(1, d))
    return z, jnp.transpose(sf.reshape(nseq, n, d // n, n), (0, 2, 3, 1))


def kernel(x_prompt, x_sample, p_prompt, p_sample, cache_k, cache_v, state_wkv, state_shift, page_table, rel_bias, norm_mix, norm_mlp, norm_ple, norm_final, att_w_qkv, att_w_o, att_lam_q1, att_lam_k1, att_lam_q2, att_lam_k2, att_subln, rw_mu, rw_w_rkv, rw_w0, rw_w1, rw_w2, rw_a0, rw_a1, rw_a2, rw_v0, rw_v1, rw_v2, rw_g1, rw_g2, rw_k_k, rw_k_a, rw_r_k, rw_ln_w, rw_ln_b, rw_w_o, mlp_w1, mlp_w2, ple_w_in, ple_w_gate):
    depth = norm_mix.shape[0]
    bp, tp, d = x_prompt.shape
    bs, ts, _ = x_sample.shape
    ts_pad = -(-ts // SUBLANES) * SUBLANES
    heads_rw = d // RW_HEAD
    page_table = page_table.astype(jnp.int32)

    xp = x_prompt.reshape(bp * tp, d)
    xs = jnp.pad(x_sample, ((0, 0), (0, ts_pad - ts), (0, 0))).reshape(bs * ts_pad, d)
    pp_all = p_prompt.reshape(depth, bp * tp, -1)
    ps_all = jnp.pad(p_sample, ((0, 0), (0, 0), (0, ts_pad - ts), (0, 0))).reshape(depth, bs * ts_pad, -1)
    flash_tabs = _flash_bias_tables(rel_bias, tp)
    decode_tabs = _decode_bias_tables(rel_bias, ts_pad, ts)

    kp_l, vp_l, ks_l, vs_l = [], [], [], []
    wkvp_l, shp_l, wkvs_l, shs_l = [], [], [], []
    vf_p = vf_s = None
    y_p = y_s = None
    for i in range(depth):
        if i % 2 == 0:
            a = i // 2
            lam_init = 0.8 - 0.6 * math.exp(-0.3 * i)
            lam = _diff_lambda(att_lam_q1[a], att_lam_k1[a], att_lam_q2[a], att_lam_k2[a], lam_init)
            wqkv = att_w_qkv[a].astype(BF16)
            wo = att_w_o[a].astype(BF16)
            qb, k, v, kb, vb = _qkv_proj(xp, norm_mix[i], wqkv)
            o = _flash_attn(qb.reshape(bp, tp, d), kb.reshape(bp, tp, d), vb.reshape(bp, tp, d),
                            lam, flash_tabs, att_subln[a], lam_init)
            xp = _proj_res(o.reshape(bp * tp, d), xp, wo)
            kp_l.append(k.reshape(bp, tp, ATT_HEADS, ATT_V_DIM))
            vp_l.append(v.reshape(bp, tp, ATT_HEADS, ATT_V_DIM))

            qb, k, v, _, _ = _qkv_proj(xs, norm_mix[i], wqkv)
            k = k.reshape(bs, ts_pad, d)
            v = v.reshape(bs, ts_pad, d)
            o = _decode_attn(qb.reshape(bs, ts_pad, d), k, v, cache_k, cache_v, a, page_table, lam,
                             decode_tabs, att_subln[a], lam_init)
            xs = _proj_res(o.reshape(bs * ts_pad, d), xs, wo)
            ks_l.append(k.reshape(bs, ts_pad, ATT_HEADS, ATT_V_DIM)[:, :ts])
            vs_l.append(v.reshape(bs, ts_pad, ATT_HEADS, ATT_V_DIM)[:, :ts])
        else:
            r = i // 2
            vres = None if r == 0 else (rw_v0[r - 1], rw_v1[r - 1], rw_v2[r - 1])
            wrkv = rw_w_rkv[r].astype(BF16)
            wo = rw_w_o[r].astype(BF16)
            lora = (rw_w0[r], rw_w1[r], rw_w2[r], rw_a0[r], rw_a1[r], rw_a2[r], rw_g1[r], rw_g2[r])
            heads = (rw_k_k[r], rw_k_a[r], rw_r_k[r].reshape(d), rw_ln_w[r], rw_ln_b[r])

            def time_mix(x2, nseq, t, shift, s0, v_first, t_valid):
                rr, kk, vv, lw, aa, gg, last = _rwkv_pre(
                    x2.reshape(nseq, t, d), shift, norm_mix[i], rw_mu[r], wrkv, *lora, vres,
                    v_first, t_valid)
                z, s_fin = _wkv(rr, kk, vv, lw, aa, s0, *heads, t_valid)
                out = _proj_res(z.reshape(nseq * t, d), x2, wo, gate2=gg.reshape(nseq * t, d))
                return out, s_fin, last.reshape(nseq, d), (vv if vres is None else v_first)

            xp, s_p, last_p, vf_p = time_mix(
                xp, bp, tp, jnp.zeros((bp, d), F32),
                jnp.zeros((bp, heads_rw, RW_HEAD, RW_HEAD), F32), vf_p, tp)
            xs, s_s, last_s, vf_s = time_mix(xs, bs, ts_pad, state_shift[r], state_wkv[r], vf_s, ts)
            wkvp_l.append(s_p)
            shp_l.append(last_p)
            wkvs_l.append(s_s)
            shs_l.append(last_s)
        final = i == depth - 1
        ffn_w = (norm_mlp[i], mlp_w1[i].astype(BF16), mlp_w2[i].astype(BF16), norm_ple[i],
                 ple_w_in[i].astype(BF16), ple_w_gate[i].astype(BF16), norm_final)
        xp, y_p = _ffn_ple(xp, pp_all[i], *ffn_w, final)
        xs, y_s = _ffn_ple(xs, ps_all[i], *ffn_w, final)

    y_prompt = y_p.reshape(bp, tp, d)
    y_sample = y_s.reshape(bs, ts_pad, d)[:, :ts]
    return (y_prompt, y_sample, jnp.stack(kp_l), jnp.stack(vp_l), jnp.stack(wkvp_l),
            jnp.stack(shp_l), jnp.stack(ks_l), jnp.stack(vs_l), jnp.stack(wkvs_l),
            jnp.stack(shs_l))
```

```python
import functools
import math

import numpy as np
import jax
import jax.numpy as jnp
from jax import lax
from jax.experimental import pallas as pl
from jax.experimental.pallas import tpu as pltpu

F32 = jnp.float32
BF16 = jnp.bfloat16
HIGHEST = lax.Precision.HIGHEST

ATT_HEADS = 8
ATT_HEAD_DIM = 64
ATT_V_DIM = 128
N_BUCKETS = 32
MAX_EXACT = 16
MAX_DISTANCE = 128
PAGE_SIZE = 128
RW_HEAD = 64
NORM_EPS = 1e-6
SUBLN_EPS = 1e-5
GN_EPS = 64e-5
NEG_INF = -1e30
LOG2E = math.log2(math.e)

LANES = 128
SUBLANES = 8
VMEM_LIMIT = 56 * 1024 * 1024
TOKEN_TILE = 512
RWKV_TILE = 256
ATT_BLOCK = 512
ATT_FAR_UNROLL = 4
ATT_ROW_CHUNK = 128
DECODE_PAGES = 4
WKV_CHUNK = 64
FF_CHUNK = 1024

NN = (((1,), (0,)), ((), ()))
NT = (((1,), (1,)), ((), ()))
TN = (((0,), (0,)), ((), ()))


def _params(n_axes):
    return pltpu.CompilerParams(dimension_semantics=("arbitrary",) * n_axes,
                                vmem_limit_bytes=VMEM_LIMIT)


def _rms(x, g, eps):
    return x * lax.rsqrt(jnp.mean(x * x, axis=-1, keepdims=True) + eps) * g


def _sigmoid(x):
    return 1.0 / (1.0 + jnp.exp(-x))


def _bdot(a, b):
    return jnp.dot(a, b, preferred_element_type=F32)


def _split(a):
    hi = a.astype(BF16)
    return hi, (a - hi.astype(F32)).astype(BF16)


def _dot3(a, b, dims=NN):
    ah, al = _split(a)
    bh, bl = _split(b)
    d = lambda x, y: lax.dot_general(x, y, dims, preferred_element_type=F32)
    return d(ah, bh) + (d(ah, bl) + d(al, bh))


def _dot3p(a, b, dims=NN):
    d = lambda x, y: lax.dot_general(x, y, dims, preferred_element_type=F32)
    return d(a[0], b[0]) + (d(a[0], b[1]) + d(a[1], b[0]))


def _full(shape):
    zeros = (0,) * len(shape)
    return pl.BlockSpec(shape, lambda *_: zeros)


def _qkv_kernel(x_ref, g_ref, w_ref, qb_ref, k_ref, v_ref, kb_ref, vb_ref):
    d = x_ref.shape[1]
    xn = _rms(x_ref[...], g_ref[...], NORM_EPS).astype(BF16)
    q = _bdot(xn, w_ref[:, 0:d])
    qb_ref[...] = (q * (ATT_HEAD_DIM ** -0.5 * LOG2E)).astype(BF16)
    k = _bdot(xn, w_ref[:, d:2 * d])
    k_ref[...] = k
    kb_ref[...] = k.astype(BF16)
    v = _bdot(xn, w_ref[:, 2 * d:3 * d])
    v_ref[...] = v
    vb_ref[...] = v.astype(BF16)


def _qkv_proj(x2, g, w_bf):
    m, d = x2.shape
    tm = min(TOKEN_TILE, m)
    row = pl.BlockSpec((tm, d), lambda i: (i, 0))
    return pl.pallas_call(
        _qkv_kernel,
        grid=(m // tm,),
        in_specs=[row, _full((1, d)), _full((d, 3 * d))],
        out_specs=[row, row, row, row, row],
        out_shape=[jax.ShapeDtypeStruct((m, d), BF16),
                   jax.ShapeDtypeStruct((m, d), F32),
                   jax.ShapeDtypeStruct((m, d), F32),
                   jax.ShapeDtypeStruct((m, d), BF16),
                   jax.ShapeDtypeStruct((m, d), BF16)],
        compiler_params=_params(1),
        name="qkv_proj",
    )(x2, g.reshape(1, d), w_bf)


def _proj_res_kernel(a_ref, res_ref, w_ref, o_ref):
    o_ref[...] = res_ref[...] + _bdot(a_ref[...].astype(BF16), w_ref[...])


def _proj_gate_res_kernel(a_ref, gate_ref, res_ref, w_ref, o_ref):
    a = (a_ref[...] * gate_ref[...]).astype(BF16)
    o_ref[...] = res_ref[...] + _bdot(a, w_ref[...])


def _proj_res(a2, res2, w_bf, gate2=None):
    m, d = res2.shape
    tm = min(TOKEN_TILE, m)
    row = pl.BlockSpec((tm, d), lambda i: (i, 0))
    if gate2 is None:
        kern, ins, specs = _proj_res_kernel, (a2, res2, w_bf), [row, row, _full((d, d))]
    else:
        kern, ins = _proj_gate_res_kernel, (a2, gate2, res2, w_bf)
        specs = [row, row, row, _full((d, d))]
    return pl.pallas_call(
        kern,
        grid=(m // tm,),
        in_specs=specs,
        out_specs=row,
        out_shape=jax.ShapeDtypeStruct((m, d), F32),
        compiler_params=_params(1),
        name="proj_res",
    )(*ins)


def _ffn_kernel(x_ref, p_ref, gm_ref, w1_ref, w2_ref, gp_ref, win_ref, wg_ref, gf_ref,
                o_ref, *y_ref):
    x = x_ref[...]
    h = _rms(x, gm_ref[...], NORM_EPS).astype(BF16)
    acc = x
    d_ff = w1_ref.shape[1]
    for c in range(d_ff // FF_CHUNK):
        cols = slice(c * FF_CHUNK, (c + 1) * FF_CHUNK)
        a = jnp.maximum(_bdot(h, w1_ref[:, cols]), 0.0)
        acc = acc + _bdot((a * a).astype(BF16), w2_ref[cols, :])
    gate = _sigmoid(_bdot(_rms(acc, gp_ref[...], NORM_EPS).astype(BF16), wg_ref[...]))
    out = acc + _bdot(p_ref[...].astype(BF16), win_ref[...]) * gate
    o_ref[...] = out
    if y_ref:
        y_ref[0][...] = _rms(out, gf_ref[...], NORM_EPS)


def _ffn_ple(x2, p2, g_mlp, w1_bf, w2_bf, g_ple, win_bf, wg_bf, g_final, emit_final):
    m, d = x2.shape
    pd = p2.shape[1]
    d_ff = w1_bf.shape[1]
    tm = min(TOKEN_TILE, m)
    row = pl.BlockSpec((tm, d), lambda i: (i, 0))
    once = dict(pipeline_mode=pl.Buffered(1))
    specs = [row, pl.BlockSpec((tm, pd), lambda i: (i, 0)), _full((1, d)),
             pl.BlockSpec((d, d_ff), lambda i: (0, 0), **once),
             pl.BlockSpec((d_ff, d), lambda i: (0, 0), **once),
             _full((1, d)),
             pl.BlockSpec((pd, d), lambda i: (0, 0), **once),
             pl.BlockSpec((d, d), lambda i: (0, 0), **once),
             _full((1, d))]
    n_out = 2 if emit_final else 1
    out = pl.pallas_call(
        _ffn_kernel,
        grid=(m // tm,),
        in_specs=specs,
        out_specs=[row] * n_out,
        out_shape=[jax.ShapeDtypeStruct((m, d), F32)] * n_out,
        compiler_params=_params(1),
        name="ffn_ple",
    )(x2, p2, g_mlp.reshape(1, d), w1_bf, w2_bf, g_ple.reshape(1, d), win_bf, wg_bf,
      g_final.reshape(1, d))
    return out if emit_final else (out[0], None)


def _bucket_np(n):
    n = np.asarray(n, np.int64)
    nf = np.maximum(n, 1).astype(np.float32)
    large = MAX_EXACT + (np.log(nf / np.float32(MAX_EXACT))
                         / np.float32(math.log(MAX_DISTANCE / MAX_EXACT))
                         * np.float32(N_BUCKETS - MAX_EXACT)).astype(np.int32)
    large = np.minimum(large, N_BUCKETS - 1)
    return np.where(n < MAX_EXACT, n, large).astype(np.int32)


def _bias_kernel(relb_ref, idx_ref, o_ref):
    h = pl.program_id(0)
    idx = idx_ref[...]
    acc = jnp.full(idx.shape, NEG_INF, F32)
    far = relb_ref[N_BUCKETS - 1, h]
    for b in range(N_BUCKETS):
        acc = jnp.where(idx == b, (relb_ref[b, h] - far) * LOG2E, acc)
    o_ref[0] = acc


def _bias_tables(rel_bias, idx):
    r, c = idx.shape
    return pl.pallas_call(
        _bias_kernel,
        grid=(ATT_HEADS,),
        in_specs=[pl.BlockSpec(memory_space=pltpu.SMEM), _full((r, c))],
        out_specs=pl.BlockSpec((1, r, c), lambda h: (h, 0, 0)),
        out_shape=jax.ShapeDtypeStruct((ATT_HEADS, r, c), F32),
        compiler_params=_params(1),
        name="bias_tables",
    )(rel_bias, jnp.asarray(idx.astype(np.int32)))


def _lam_kernel(q1_ref, k1_ref, q2_ref, k2_ref, o_ref, *, lam_init):
    e1 = jnp.exp(jnp.sum(q1_ref[...] * k1_ref[...], axis=-1, keepdims=True))
    e2 = jnp.exp(jnp.sum(q2_ref[...] * k2_ref[...], axis=-1, keepdims=True))
    o_ref[...] = jnp.broadcast_to(e1 - e2 + lam_init, o_ref.shape)


def _diff_lambda(lq1, lk1, lq2, lk2, lam_init):
    n = lq1.shape[0]
    out = pl.pallas_call(
        functools.partial(_lam_kernel, lam_init=lam_init),
        in_specs=[_full((1, n))] * 4,
        out_specs=_full((1, LANES)),
        out_shape=jax.ShapeDtypeStruct((1, LANES), F32),
        grid=(1,),
        compiler_params=_params(1),
        name="diff_lambda",
    )(lq1.reshape(1, n), lk1.reshape(1, n), lq2.reshape(1, n), lk2.reshape(1, n))
    return out[0, :1]


def _online_softmax(s, v1, m_ref, acc_ref, rows):
    tiles = [s[:, c * LANES:(c + 1) * LANES] for c in range(s.shape[1] // LANES)]
    m_prev = m_ref[rows, :]
    row_max = jnp.max(functools.reduce(jnp.maximum, tiles), axis=1, keepdims=True)
    m_new = jnp.maximum(m_prev, row_max)
    alpha = jnp.exp2(m_prev - m_new)
    p = jnp.concatenate([jnp.exp2(t - m_new).astype(BF16) for t in tiles], axis=1)
    acc_ref[rows, :] = jnp.concatenate([alpha, alpha], axis=1) * acc_ref[rows, :] + _bdot(p, v1)
    m_ref[rows, :] = m_new


def _flash_kernel(lam_ref, q_ref, k_ref, v_ref, b0_ref, b1_ref, g_ref, o_ref,
                  qs_ref, v1_ref, m_ref, acc_ref, *, blk, chunk, lam_init):
    i = pl.program_id(2)

    @pl.when(i == 0)
    def _():
        v1_ref[:, 0:ATT_V_DIM] = v_ref[0]
        v1_ref[:, ATT_V_DIM:2 * ATT_V_DIM] = jnp.ones((v1_ref.shape[0], ATT_V_DIM), BF16)

    q = q_ref[0]
    lane = lax.broadcasted_iota(jnp.int32, q.shape, 1)
    zero = jnp.zeros_like(q)
    qs_ref[0:blk, :] = jnp.where(lane < ATT_HEAD_DIM, q, zero)
    qs_ref[blk:2 * blk, :] = jnp.where(lane >= ATT_HEAD_DIM, q, zero)
    m_ref[...] = jnp.full(m_ref.shape, NEG_INF, F32)
    acc_ref[...] = jnp.zeros(acc_ref.shape, F32)

    def step(j, bias):
        cols = pl.ds(pl.multiple_of(j * blk, blk), blk)
        kb = k_ref[0, cols, :]
        v1 = v1_ref[cols, :]
        for c in range(2 * blk // chunk):
            rows = slice(c * chunk, (c + 1) * chunk)
            s = lax.dot_general(qs_ref[rows, :], kb, NT, preferred_element_type=F32)
            if bias is not None:
                r0 = (c * chunk) % blk
                s = s + bias[r0:r0 + chunk, :]
            _online_softmax(s, v1, m_ref, acc_ref, rows)

    n_far = jnp.maximum(i - 1, 0)

    def far_body(jj, carry):
        for u in range(ATT_FAR_UNROLL):
            step(jj * ATT_FAR_UNROLL + u, None)
        return carry

    lax.fori_loop(0, n_far // ATT_FAR_UNROLL, far_body, 0)
    done = (n_far // ATT_FAR_UNROLL) * ATT_FAR_UNROLL
    bit = ATT_FAR_UNROLL // 2
    while bit:
        take = (n_far - done) & bit

        @pl.when(take != 0)
        def _(done=done, bit=bit):
            for u in range(bit):
                step(done + u, None)

        done = done + take
        bit //= 2

    @pl.when(i >= 1)
    def _():
        step(i - 1, b1_ref[0])
        step(i, b0_ref[0])

    @pl.when(i == 0)
    def _():
        step(0, b0_ref[0])

    acc = acc_ref[...]
    o12 = acc[:, 0:ATT_V_DIM] / acc[:, ATT_V_DIM:2 * ATT_V_DIM]
    o = o12[0:blk, :] - lam_ref[0] * o12[blk:2 * blk, :]
    y = _rms(o, g_ref[...], SUBLN_EPS) * (1.0 - lam_init)
    o_ref[0] = y.astype(BF16)


def _flash_bias_tables(rel_bias, t):
    blk = min(ATT_BLOCK, t)
    assert blk >= MAX_DISTANCE and t % blk == 0
    r = np.arange(blk)[:, None]
    c = np.arange(blk)[None, :]
    idx_diag = np.where(r >= c, _bucket_np(np.maximum(r - c, 0)), -1)
    idx_sub = _bucket_np(blk + r - c)
    assert (_bucket_np(np.arange(blk + 1, 4 * blk)) == N_BUCKETS - 1).all()
    return _bias_tables(rel_bias, np.concatenate([idx_diag, idx_sub], axis=0))


def _flash_attn(qb, kb, vb, lam, tabs, subln_g, lam_init):
    b, t, d = qb.shape
    blk = min(ATT_BLOCK, t)
    qspec = pl.BlockSpec((1, blk, ATT_V_DIM), lambda bi, h, i: (bi, i, h))
    kvspec = pl.BlockSpec((1, t, ATT_V_DIM), lambda bi, h, i: (bi, 0, h))
    return pl.pallas_call(
        functools.partial(_flash_kernel, blk=blk, chunk=min(ATT_ROW_CHUNK, blk),
                          lam_init=lam_init),
        grid=(b, ATT_HEADS, t // blk),
        in_specs=[pl.BlockSpec(memory_space=pltpu.SMEM),
                  qspec, kvspec, kvspec,
                  pl.BlockSpec((1, blk, blk), lambda bi, h, i: (h, 0, 0)),
                  pl.BlockSpec((1, blk, blk), lambda bi, h, i: (h, 1, 0)),
                  _full((1, ATT_V_DIM))],
        out_specs=qspec,
        out_shape=jax.ShapeDtypeStruct((b, t, d), BF16),
        scratch_shapes=[pltpu.VMEM((2 * blk, ATT_V_DIM), BF16),
                        pltpu.VMEM((t, 2 * ATT_V_DIM), BF16),
                        pltpu.VMEM((2 * blk, LANES), F32),
                        pltpu.VMEM((2 * blk, 2 * ATT_V_DIM), F32)],
        compiler_params=_params(3),
        name="flash_diff_attn",
    )(lam, qb, kb, vb, tabs, tabs, subln_g.reshape(1, ATT_V_DIM))


def _decode_kernel(pt_ref, lam_ref, q_ref, *refs, tq, lam_init):
    del pt_ref
    ck_refs = refs[0:DECODE_PAGES]
    cv_refs = refs[DECODE_PAGES:2 * DECODE_PAGES]
    (kn_ref, vn_ref, tab_ref, tabn_ref, g_ref, o_ref,
     qa_ref, m_ref, l_ref, acc_ref) = refs[2 * DECODE_PAGES:]
    p = pl.program_id(1)
    n_groups = pl.num_programs(1) - 1
    grp = 2 * tq

    @pl.when(p == 0)
    def _():
        for h in range(ATT_HEADS):
            qh = q_ref[0, :, ATT_V_DIM * h:ATT_V_DIM * (h + 1)].astype(F32)
            lane = lax.broadcasted_iota(jnp.int32, qh.shape, 1)
            qa_ref[h * grp:h * grp + tq, :] = jnp.where(lane < ATT_HEAD_DIM, qh, 0.0)
            qa_ref[h * grp + tq:(h + 1) * grp, :] = jnp.where(lane >= ATT_HEAD_DIM, qh, 0.0)
        m_ref[...] = jnp.full(m_ref.shape, NEG_INF, F32)
        l_ref[...] = jnp.zeros(l_ref.shape, F32)
        acc_ref[...] = jnp.zeros(acc_ref.shape, F32)

    def step(pages):
        qa = qa_ref[...].astype(BF16)
        tiles = []
        for k3, _, tab in pages:
            k2 = k3.reshape(-1, ATT_V_DIM).astype(BF16)
            s = lax.dot_general(qa, k2, NT, preferred_element_type=F32) + tab
            tiles += [s[:, c * LANES:(c + 1) * LANES] for c in range(-(-s.shape[1] // LANES))]
        m_prev = m_ref[...]
        row_max = jnp.max(functools.reduce(jnp.maximum, tiles), axis=1, keepdims=True)
        m_new = jnp.maximum(m_prev, row_max)
        alpha = jnp.exp2(m_prev - m_new)
        ps = [jnp.exp2(t - m_new[:, 0:t.shape[1]]) for t in tiles]
        row_sum = jnp.sum(functools.reduce(jnp.add, ps), axis=1, keepdims=True)
        l_ref[...] = alpha * l_ref[...] + row_sum
        acc = alpha * acc_ref[...]
        per_page = len(ps) // len(pages)
        for n, (_, v3, _) in enumerate(pages):
            v2 = v3.reshape(-1, ATT_V_DIM).astype(BF16)
            pb = jnp.concatenate([x.astype(BF16) for x in ps[n * per_page:(n + 1) * per_page]],
                                 axis=1)
            acc = acc + _bdot(pb, v2)
        acc_ref[...] = acc
        m_ref[...] = m_new

    def cached(last_kind):
        kinds = [0] * (DECODE_PAGES - 1) + [last_kind]
        return [(ck[...], cv[...], tab_ref[kind]) for ck, cv, kind in zip(ck_refs, cv_refs, kinds)]

    @pl.when(p < n_groups - 1)
    def _():
        step(cached(0))

    @pl.when(p == n_groups - 1)
    def _():
        step(cached(1))

    @pl.when(p == n_groups)
    def _():
        step([(kn_ref[0], vn_ref[0], tabn_ref[...])])
        o12 = acc_ref[...] / l_ref[...]
        for h in range(ATT_HEADS):
            o = o12[h * grp:h * grp + tq, :] - lam_ref[0] * o12[h * grp + tq:(h + 1) * grp, :]
            y = _rms(o, g_ref[...], SUBLN_EPS) * (1.0 - lam_init)
            o_ref[0, :, ATT_V_DIM * h:ATT_V_DIM * (h + 1)] = y


def _decode_bias_tables(rel_bias, tq, t_valid):
    tt = np.arange(tq)[:, None]
    cc = np.arange(PAGE_SIZE)[None, :]
    idx_far = np.full((tq, PAGE_SIZE), N_BUCKETS - 1)
    idx_last = _bucket_np(PAGE_SIZE - cc + tt)
    assert (_bucket_np(PAGE_SIZE + 1 + 0 * cc + tt) == N_BUCKETS - 1).all()
    idx_new = np.where((cc <= tt) & (cc < t_valid), _bucket_np(np.maximum(tt - cc, 0)), -1)
    tabs = _bias_tables(rel_bias, np.concatenate([idx_far, idx_last, idx_new], axis=0))
    h = ATT_HEADS
    tb = jnp.transpose(tabs.reshape(h, 3, tq, PAGE_SIZE), (1, 0, 2, 3))
    tb = jnp.broadcast_to(tb[:, :, None, :, :, None], (3, h, 2, tq, PAGE_SIZE, h))
    own = np.eye(h, dtype=bool)[None, :, None, None, None, :]
    tb = jnp.where(own, tb, NEG_INF)
    cached = tb[0:2].reshape(2, h * 2 * tq, PAGE_SIZE * h)
    new = tb[2, :, :, :, 0:tq, :].reshape(h * 2 * tq, tq * h)
    return cached, new


def _decode_attn(qb, k, v, cache_k, cache_v, layer, page_table, lam, tabs, subln_g, lam_init):
    bs, tq, d = qb.shape
    n_pages = page_table.shape[1]
    heads = (ATT_HEADS, ATT_V_DIM)
    kn = k.reshape(bs, tq, *heads)
    vn = v.reshape(bs, tq, *heads)
    tabs, tabs_new = tabs
    rows = ATT_HEADS * 2 * tq
    assert n_pages % DECODE_PAGES == 0
    n_groups = n_pages // DECODE_PAGES

    def page_map(which):
        def index(bi, p, pt):
            return (layer, pt[bi, DECODE_PAGES * jnp.minimum(p, n_groups - 1) + which], 0, 0, 0)
        return index

    seq3 = lambda bi, p, pt: (bi, 0, 0)
    seq4 = lambda bi, p, pt: (bi, 0, 0, 0)
    page = lambda which: pl.BlockSpec((None, None, PAGE_SIZE) + heads, page_map(which))
    new = pl.BlockSpec((1, tq) + heads, seq4)
    grid_spec = pltpu.PrefetchScalarGridSpec(
        num_scalar_prefetch=1,
        grid=(bs, n_groups + 1),
        in_specs=[pl.BlockSpec(memory_space=pltpu.SMEM),
                  pl.BlockSpec((1, tq, d), seq3),
                  *[page(n) for n in range(DECODE_PAGES)] * 2, new, new,
                  pl.BlockSpec(tabs.shape, lambda bi, p, pt: (0, 0, 0)),
                  pl.BlockSpec(tabs_new.shape, lambda bi, p, pt: (0, 0)),
                  pl.BlockSpec((1, ATT_V_DIM), lambda bi, p, pt: (0, 0))],
        out_specs=pl.BlockSpec((1, tq, d), seq3),
        scratch_shapes=[pltpu.VMEM((rows, ATT_V_DIM), F32),
                        pltpu.VMEM((rows, LANES), F32),
                        pltpu.VMEM((rows, LANES), F32),
                        pltpu.VMEM((rows, ATT_V_DIM), F32)])
    return pl.pallas_call(
        functools.partial(_decode_kernel, tq=tq, lam_init=lam_init),
        grid_spec=grid_spec,
        out_shape=jax.ShapeDtypeStruct((bs, tq, d), F32),
        compiler_params=_params(2),
        name="decode_diff_attn",
    )(page_table, lam, qb, *[cache_k] * DECODE_PAGES, *[cache_v] * DECODE_PAGES, kn, vn, tabs,
      tabs_new, subln_g.reshape(1, ATT_V_DIM))


def _rwkv_pre_kernel(*refs, has_vres, t_last):
    if has_vres:
        (x_ref, sh_ref, gm_ref, mu_ref, wrkv_ref, w0_ref, w1_ref, w2_ref, a0_ref, a1_ref, a2_ref,
         g1_ref, g2_ref, vf_ref, v0_ref, v1_ref, v2_ref,
         r_ref, k_ref, v_ref, lw_ref, a_ref, g_ref, last_ref, carry_ref) = refs
    else:
        (x_ref, sh_ref, gm_ref, mu_ref, wrkv_ref, w0_ref, w1_ref, w2_ref, a0_ref, a1_ref, a2_ref,
         g1_ref, g2_ref,
         r_ref, k_ref, v_ref, lw_ref, a_ref, g_ref, last_ref, carry_ref) = refs
    ti = pl.program_id(1)
    tm = x_ref.shape[1]
    xn = _rms(x_ref[0], gm_ref[...], NORM_EPS)

    @pl.when(ti == 0)
    def _():
        carry_ref[...] = sh_ref[0]

    rolled = pltpu.roll(xn, 1, axis=0)
    rid = lax.broadcasted_iota(jnp.int32, xn.shape, 0)
    x_prev = jnp.where(rid == 0, carry_ref[...], rolled)
    carry_ref[...] = xn[tm - 1:tm, :]

    @pl.when(ti == pl.num_programs(1) - 1)
    def _():
        last_ref[0] = xn[t_last:t_last + 1, :]

    xx = x_prev - xn
    mix = lambda n: (xn + xx * mu_ref[n:n + 1, :]).astype(BF16)
    xr, xw, xk, xv, xa, xg = [mix(n) for n in range(6)]
    r_ref[0] = _bdot(xr, wrkv_ref[0])
    k_ref[0] = _bdot(xk, wrkv_ref[1])
    v = _bdot(xv, wrkv_ref[2])
    if has_vres:
        lo = _bdot(_bdot(xv, v1_ref[...]).astype(BF16), v2_ref[...])
        v = v + (vf_ref[0] - v) * _sigmoid(v0_ref[...] + lo)
    v_ref[0] = v
    wl = w0_ref[...] + _bdot(jnp.tanh(_bdot(xw, w1_ref[...])).astype(BF16), w2_ref[...])
    z = -wl
    softplus = jnp.maximum(z, 0.0) + jnp.log(1.0 + jnp.exp(-jnp.abs(z)))
    lw_ref[0] = -jnp.exp(-softplus - 0.5)
    a_ref[0] = _sigmoid(a0_ref[...] + _bdot(_bdot(xa, a1_ref[...]).astype(BF16), a2_ref[...]))
    g_ref[0] = _bdot(_sigmoid(_bdot(xg, g1_ref[...])).astype(BF16), g2_ref[...])


def _pad_lora(w_in, w_out):
    n = w_in.shape[1]
    n_pad = -(-n // LANES) * LANES - n
    return (jnp.pad(w_in, ((0, 0), (0, n_pad))).astype(BF16),
            jnp.pad(w_out, ((0, n_pad), (0, 0))).astype(BF16))


def _rwkv_pre(x3, shift, g_mix, mu, wrkv_bf, w0, w1, w2, a0, a1, a2, g1, g2, vres, v_first, t_valid):
    nseq, t, d = x3.shape
    tm = min(RWKV_TILE, t)
    nt = t // tm
    t_last = (t_valid - 1) - (nt - 1) * tm
    tile = pl.BlockSpec((1, tm, d), lambda s, i: (s, i, 0))
    vec = _full((1, d))
    w1p, w2p = _pad_lora(w1, w2)
    a1p, a2p = _pad_lora(a1, a2)
    g1p, g2p = _pad_lora(g1, g2)
    ins = [x3, shift.reshape(nseq, 1, d), g_mix.reshape(1, d), mu, wrkv_bf,
           w0.reshape(1, d), w1p, w2p, a0.reshape(1, d), a1p, a2p, g1p, g2p]
    specs = [tile, pl.BlockSpec((1, 1, d), lambda s, i: (s, 0, 0)), vec, _full(mu.shape),
             pl.BlockSpec(wrkv_bf.shape, lambda s, i: (0, 0, 0), pipeline_mode=pl.Buffered(1)),
             vec, _full(w1p.shape), _full(w2p.shape),
             vec, _full(a1p.shape), _full(a2p.shape), _full(g1p.shape), _full(g2p.shape)]
    if vres is not None:
        v0, v1, v2 = vres
        v1p, v2p = _pad_lora(v1, v2)
        ins += [v_first, v0.reshape(1, d), v1p, v2p]
        specs += [tile, vec, _full(v1p.shape), _full(v2p.shape)]
    big = jax.ShapeDtypeStruct((nseq, t, d), F32)
    outs = pl.pallas_call(
        functools.partial(_rwkv_pre_kernel, has_vres=vres is not None, t_last=t_last),
        grid=(nseq, nt),
        in_specs=specs,
        out_specs=[tile] * 6 + [pl.BlockSpec((1, 1, d), lambda s, i: (s, 0, 0))],
        out_shape=[big] * 6 + [jax.ShapeDtypeStruct((nseq, 1, d), F32)],
        scratch_shapes=[pltpu.VMEM((1, d), F32)],
        compiler_params=_params(2),
        name="rwkv_pre",
    )(*ins)
    return outs


def _lane_half(shape, period=None):
    period = period or shape[1]
    lane = lax.broadcasted_iota(jnp.int32, shape, 1)
    return (lane % period) < period // 2


def _bd_rows(x, period=None):
    first = _lane_half(x.shape, period)
    zero = jnp.zeros_like(x)
    return jnp.concatenate([jnp.where(first, x, zero), jnp.where(first, zero, x)], axis=0)


def _bd_split(x, period=None):
    return tuple(_bd_rows(part, period) for part in _split(x))


def _pair_diag(full):
    n = full.shape[0] // 2
    return jnp.where(_lane_half((n, 2 * n)), full[0:n, :], full[n:2 * n, :])


def _pair_sum(x):
    first = _lane_half(x.shape)
    lo = jnp.sum(jnp.where(first, x, 0.0), axis=-1, keepdims=True)
    hi = jnp.sum(jnp.where(first, 0.0, x), axis=-1, keepdims=True)
    return jnp.where(first, lo, hi)


def _wkv_prep_kernel(r_ref, k_ref, v_ref, lw_ref, a_ref, kk_ref, ka_ref, rk_ref,
                     rp_ref, yc_ref, bn_ref, g_ref, h_ref, *, t_valid):
    ln = r_ref.shape[1]
    d = r_ref.shape[2]
    n = RW_HEAD
    pairs = range(d // (2 * n))
    sl = [slice(p * 2 * n, (p + 1) * 2 * n) for p in pairs]
    ri = lax.broadcasted_iota(jnp.int32, (ln, ln), 0)
    ci = lax.broadcasted_iota(jnp.int32, (ln, ln), 1)
    incl = (ri >= ci).astype(F32)
    incl2 = jnp.concatenate([incl, incl], axis=1)
    strict2 = jnp.concatenate([(ri > ci).astype(F32)] * 2, axis=1)
    eye_l2 = jnp.concatenate([(ri == ci).astype(F32)] * 2, axis=1)
    eye_n = (lax.broadcasted_iota(jnp.int32, (n, n), 0)
             == lax.broadcasted_iota(jnp.int32, (n, n), 1)).astype(F32)
    r, k, v, lw, a = r_ref[0], k_ref[0], v_ref[0], lw_ref[0], a_ref[0]
    if t_valid is not None:
        t_glob = pl.program_id(1) * ln + lax.broadcasted_iota(jnp.int32, (ln, 1), 0)
        live = (t_glob < t_valid).astype(F32)
        r, k, v, lw, a = r * live, k * live, v * live, lw * live, a * live

    cum = jnp.dot(incl, lw, precision=HIGHEST, preferred_element_type=F32)
    cum_last = cum[ln - 1:ln, :]
    e_neg = jnp.exp(-cum)
    e_out = jnp.exp(cum_last - cum)
    w_last = jnp.exp(cum_last)
    rt = r * jnp.exp(cum)
    kmod = k * (1.0 + (a - 1.0) * ka_ref[...])
    kt = kmod * e_neg
    ko = kmod * e_out
    kk0 = k * kk_ref[...]
    rkr = r * kmod * rk_ref[...]
    kk = jnp.concatenate(
        [kk0[:, sl[p]] / jnp.maximum(jnp.sqrt(_pair_sum(kk0[:, sl[p]] * kk0[:, sl[p]])), 1e-12)
         for p in pairs], axis=1)
    bn_ref[0] = jnp.concatenate([_pair_sum(rkr[:, sl[p]]) for p in pairs], axis=1) * v
    b_ = kk * a
    at_all = -kk * jnp.exp(cum - lw)
    bt_all = b_ * e_neg
    bo_all = b_ * e_out

    at = [at_all[:, sl[p]] for p in pairs]
    bo = [bo_all[:, sl[p]] for p in pairs]
    vp = [v[:, sl[p]] for p in pairs]
    cat0 = lambda u, w: tuple(jnp.concatenate([x, y], axis=0) for x, y in zip(u, w))
    cat1 = lambda u, w: tuple(jnp.concatenate([x, y], axis=1) for x, y in zip(u, w))
    vbd = [_bd_split(vp[p]) for p in pairs]
    sc = [_dot3p(_split(jnp.concatenate([at[p], rt[:, sl[p]]], axis=0)),
                 cat0(_bd_split(bt_all[:, sl[p]]), _bd_split(kt[:, sl[p]])), NT)
          for p in pairs]
    ab = [sc[p][0:ln, 0:2 * ln] * strict2 for p in pairs]
    ak = [sc[p][0:ln, 2 * ln:4 * ln] * strict2 for p in pairs]
    rb = [sc[p][ln:2 * ln, 0:2 * ln] * incl2 for p in pairs]
    rk = [sc[p][ln:2 * ln, 2 * ln:4 * ln] * incl2 for p in pairs]
    tinv = [eye_l2 + ab[p] for p in pairs]
    pw = ab
    pw_s = [_split(pw[p]) for p in pairs]
    for _ in range(int(math.log2(ln)) - 1):
        pw = [_dot3p(pw_s[p], tuple(_bd_rows(x) for x in pw_s[p])) for p in pairs]
        pw_s = [_split(pw[p]) for p in pairs]
        tinv = [tinv[p] + _dot3p(pw_s[p], _bd_split(tinv[p])) for p in pairs]
    akv = [_dot3p(_split(ak[p]), vbd[p]) for p in pairs]
    x = [_dot3p(_split(tinv[p]), cat1(_bd_split(at[p]), _bd_split(akv[p])))
         for p in pairs]
    zero = (jnp.zeros((2 * ln, 2 * n), BF16),) * 2
    y = [_dot3p(_split(jnp.concatenate([rb[p], rk[p]], axis=1)),
                cat0(_bd_split(x[p], 2 * n), cat1(zero, vbd[p])))
         for p in pairs]
    rp_ref[0] = rt + jnp.concatenate([y[p][:, 0:2 * n] for p in pairs], axis=1)
    yc_ref[0] = jnp.concatenate([y[p][:, 2 * n:4 * n] for p in pairs], axis=1)
    gm = [_pair_diag(_dot3(bo[p], x[p][:, 0:2 * n], TN)) for p in pairs]
    hm = [_pair_diag(_dot3(jnp.concatenate([bo[p], ko[:, sl[p]]], axis=0),
                           jnp.concatenate([x[p][:, 2 * n:4 * n], vp[p]], axis=0), TN))
          for p in pairs]
    g_ref[0, 0] = jnp.concatenate([eye_n] * (d // n), axis=1) * w_last + jnp.concatenate(gm, axis=1)
    h_ref[0, 0] = jnp.concatenate(hm, axis=1)


def _wkv_scan_kernel(rp_ref, yc_ref, bn_ref, g_ref, h_ref, s0_ref, lnw_ref, lnb_ref,
                     z_ref, sf_ref, s_ref):
    n = RW_HEAD
    d = rp_ref.shape[2]
    pairs = range(d // (2 * n))
    sl = [slice(p * 2 * n, (p + 1) * 2 * n) for p in pairs]

    @pl.when(pl.program_id(1) == 0)
    def _():
        s_ref[...] = s0_ref[0]

    s0 = [_bd_split(s_ref[:, sl[p]]) for p in pairs]
    y = [_dot3p(_split(rp_ref[0, :, sl[p]]), s0[p]) + yc_ref[0, :, sl[p]] for p in pairs]
    s1 = [_dot3p(_split(g_ref[0, 0, :, sl[p]]), s0[p]) + h_ref[0, 0, :, sl[p]] for p in pairs]
    s_new = jnp.concatenate(s1, axis=1)
    s_ref[...] = s_new
    sf_ref[0] = s_new
    yc = [y[p] - _pair_sum(y[p]) * (1.0 / n) for p in pairs]
    var = [_pair_sum(yc[p] * yc[p]) * (1.0 / n) for p in pairs]
    yn = jnp.concatenate([yc[p] * lax.rsqrt(var[p] + GN_EPS) for p in pairs], axis=1)
    z_ref[0] = yn * lnw_ref[...] + lnb_ref[...] + bn_ref[0]


def _wkv(r, k, v, lw, a, s0, k_k, k_a, r_k, ln_w, ln_b, t_valid):
    nseq, t, d = r.shape
    n = RW_HEAD
    ln = min(WKV_CHUNK, t)
    nc = t // ln
    tile = pl.BlockSpec((1, ln, d), lambda s, c: (s, c, 0))
    vec = _full((1, d))
    mat = pl.BlockSpec((1, 1, n, d), lambda s, c: (s, c, 0, 0))
    big = jax.ShapeDtypeStruct((nseq, t, d), F32)
    mats = jax.ShapeDtypeStruct((nseq, nc, n, d), F32)
    rp, yc, bn, g, hm = pl.pallas_call(
        functools.partial(_wkv_prep_kernel, t_valid=None if t_valid == t else t_valid),
        grid=(nseq, nc),
        in_specs=[tile] * 5 + [vec] * 3,
        out_specs=[tile] * 3 + [mat] * 2,
        out_shape=[big] * 3 + [mats] * 2,
        compiler_params=_params(2),
        name="wkv_prep",
    )(r, k, v, lw, a, k_k.reshape(1, d), k_a.reshape(1, d), r_k.reshape(1, d))
    s0_l = jnp.transpose(s0, (0, 3, 1, 2)).reshape(nseq, n, d)
    state = pl.BlockSpec((1, n, d), lambda s, c: (s, 0, 0))
    z, sf = pl.pallas_call(
        _wkv_scan_kernel,
        grid=(nseq, nc),
        in_specs=[tile] * 3 + [mat] * 2 + [state, vec, vec],
        out_specs=[tile, state],
        out_shape=[big, jax.ShapeDtypeStruct((nseq, n, d), F32)],
        scratch_shapes=[pltpu.VMEM((n, d), F32)],
        compiler_params=_params(2),
        name="wkv_scan",
    )(rp, yc, bn, g, hm, s0_l, ln_w.reshape(1, d), ln_b.reshape(1, d))
    return z, jnp.transpose(sf.reshape(nseq, n, d // n, n), (0, 2, 3, 1))


def kernel(x_prompt, x_sample, p_prompt, p_sample, cache_k, cache_v, state_wkv, state_shift, page_table, rel_bias, norm_mix, norm_mlp, norm_ple, norm_final, att_w_qkv, att_w_o, att_lam_q1, att_lam_k1, att_lam_q2, att_lam_k2, att_subln, rw_mu, rw_w_rkv, rw_w0, rw_w1, rw_w2, rw_a0, rw_a1, rw_a2, rw_v0, rw_v1, rw_v2, rw_g1, rw_g2, rw_k_k, rw_k_a, rw_r_k, rw_ln_w, rw_ln_b, rw_w_o, mlp_w1, mlp_w2, ple_w_in, ple_w_gate):
    depth = norm_mix.shape[0]
    bp, tp, d = x_prompt.shape
    bs, ts, _ = x_sample.shape
    ts_pad = -(-ts // SUBLANES) * SUBLANES
    heads_rw = d // RW_HEAD
    page_table = page_table.astype(jnp.int32)

    xp = x_prompt.reshape(bp * tp, d)
    xs = jnp.pad(x_sample, ((0, 0), (0, ts_pad - ts), (0, 0))).reshape(bs * ts_pad, d)
    pp_all = p_prompt.reshape(depth, bp * tp, -1)
    ps_all = jnp.pad(p_sample, ((0, 0), (0, 0), (0, ts_pad - ts), (0, 0))).reshape(depth, bs * ts_pad, -1)
    flash_tabs = _flash_bias_tables(rel_bias, tp)
    decode_tabs = _decode_bias_tables(rel_bias, ts_pad, ts)

    kp_l, vp_l, ks_l, vs_l = [], [], [], []
    wkvp_l, shp_l, wkvs_l, shs_l = [], [], [], []
    vf_p = vf_s = None
    y_p = y_s = None
    for i in range(depth):
        if i % 2 == 0:
            a = i // 2
            lam_init = 0.8 - 0.6 * math.exp(-0.3 * i)
            lam = _diff_lambda(att_lam_q1[a], att_lam_k1[a], att_lam_q2[a], att_lam_k2[a], lam_init)
            wqkv = att_w_qkv[a].astype(BF16)
            wo = att_w_o[a].astype(BF16)
            qb, k, v, kb, vb = _qkv_proj(xp, norm_mix[i], wqkv)
            o = _flash_attn(qb.reshape(bp, tp, d), kb.reshape(bp, tp, d), vb.reshape(bp, tp, d),
                            lam, flash_tabs, att_subln[a], lam_init)
            xp = _proj_res(o.reshape(bp * tp, d), xp, wo)
            kp_l.append(k.reshape(bp, tp, ATT_HEADS, ATT_V_DIM))
            vp_l.append(v.reshape(bp, tp, ATT_HEADS, ATT_V_DIM))

            qb, k, v, _, _ = _qkv_proj(xs, norm_mix[i], wqkv)
            k = k.reshape(bs, ts_pad, d)
            v = v.reshape(bs, ts_pad, d)
            o = _decode_attn(qb.reshape(bs, ts_pad, d), k, v, cache_k, cache_v, a, page_table, lam,
                             decode_tabs, att_subln[a], lam_init)
            xs = _proj_res(o.reshape(bs * ts_pad, d), xs, wo)
            ks_l.append(k.reshape(bs, ts_pad, ATT_HEADS, ATT_V_DIM)[:, :ts])
            vs_l.append(v.reshape(bs, ts_pad, ATT_HEADS, ATT_V_DIM)[:, :ts])
        else:
            r = i // 2
            vres = None if r == 0 else (rw_v0[r - 1], rw_v1[r - 1], rw_v2[r - 1])
            wrkv = rw_w_rkv[r].astype(BF16)
            wo = rw_w_o[r].astype(BF16)
            lora = (rw_w0[r], rw_w1[r], rw_w2[r], rw_a0[r], rw_a1[r], rw_a2[r], rw_g1[r], rw_g2[r])
            heads = (rw_k_k[r], rw_k_a[r], rw_r_k[r].reshape(d), rw_ln_w[r], rw_ln_b[r])

            def time_mix(x2, nseq, t, shift, s0, v_first, t_valid):
                rr, kk, vv, lw, aa, gg, last = _rwkv_pre(
                    x2.reshape(nseq, t, d), shift, norm_mix[i], rw_mu[r], wrkv, *lora, vres,
                    v_first, t_valid)
                z, s_fin = _wkv(rr, kk, vv, lw, aa, s0, *heads, t_valid)
                out = _proj_res(z.reshape(nseq * t, d), x2, wo, gate2=gg.reshape(nseq * t, d))
                return out, s_fin, last.reshape(nseq, d), (vv if vres is None else v_first)

            xp, s_p, last_p, vf_p = time_mix(
                xp, bp, tp, jnp.zeros((bp, d), F32),
                jnp.zeros((bp, heads_rw, RW_HEAD, RW_HEAD), F32), vf_p, tp)
            xs, s_s, last_s, vf_s = time_mix(xs, bs, ts_pad, state_shift[r], state_wkv[r], vf_s, ts)
            wkvp_l.append(s_p)
            shp_l.append(last_p)
            wkvs_l.append(s_s)
            shs_l.append(last_s)
        final = i == depth - 1
        ffn_w = (norm_mlp[i], mlp_w1[i].astype(BF16), mlp_w2[i].astype(BF16), norm_ple[i],
                 ple_w_in[i].astype(BF16), ple_w_gate[i].astype(BF16), norm_final)
        xp, y_p = _ffn_ple(xp, pp_all[i], *ffn_w, final)
        xs, y_s = _ffn_ple(xs, ps_all[i], *ffn_w, final)

    y_prompt = y_p.reshape(bp, tp, d)
    y_sample = y_s.reshape(bs, ts_pad, d)[:, :ts]
    return (y_prompt, y_sample, jnp.stack(kp_l), jnp.stack(vp_l), jnp.stack(wkvp_l),
            jnp.stack(shp_l), jnp.stack(ks_l), jnp.stack(vs_l), jnp.stack(wkvs_l),
            jnp.stack(shs_l))
```

```python
import functools
import math

import numpy as np
import jax
import jax.numpy as jnp
from jax import lax
from jax.experimental import pallas as pl
from jax.experimental.pallas import tpu as pltpu

F32 = jnp.float32
BF16 = jnp.bfloat16
HIGHEST = lax.Precision.HIGHEST

ATT_HEADS = 8
ATT_HEAD_DIM = 64
ATT_V_DIM = 128
N_BUCKETS = 32
MAX_EXACT = 16
MAX_DISTANCE = 128
PAGE_SIZE = 128
RW_HEAD = 64
NORM_EPS = 1e-6
SUBLN_EPS = 1e-5
GN_EPS = 64e-5
NEG_INF = -1e30
LOG2E = math.log2(math.e)

LANES = 128
SUBLANES = 8
VMEM_LIMIT = 56 * 1024 * 1024
TOKEN_TILE = 512
RWKV_TILE = 256
ATT_BLOCK = 512
ATT_FAR_UNROLL = 8
ATT_ROW_CHUNK = 128
DECODE_PAGES = 8
WKV_CHUNK = 64
FF_CHUNK = 1024

NN = (((1,), (0,)), ((), ()))
NT = (((1,), (1,)), ((), ()))
TN = (((0,), (0,)), ((), ()))


def _params(n_axes):
    return pltpu.CompilerParams(dimension_semantics=("arbitrary",) * n_axes,
                                vmem_limit_bytes=VMEM_LIMIT)


def _rms(x, g, eps):
    return x * lax.rsqrt(jnp.mean(x * x, axis=-1, keepdims=True) + eps) * g


def _sigmoid(x):
    return 1.0 / (1.0 + jnp.exp(-x))


def _bdot(a, b):
    return jnp.dot(a, b, preferred_element_type=F32)


def _split(a):
    hi = a.astype(BF16)
    return hi, (a - hi.astype(F32)).astype(BF16)


def _dot3(a, b, dims=NN):
    ah, al = _split(a)
    bh, bl = _split(b)
    d = lambda x, y: lax.dot_general(x, y, dims, preferred_element_type=F32)
    return d(ah, bh) + (d(ah, bl) + d(al, bh))


def _dot3p(a, b, dims=NN):
    d = lambda x, y: lax.dot_general(x, y, dims, preferred_element_type=F32)
    return d(a[0], b[0]) + (d(a[0], b[1]) + d(a[1], b[0]))


def _full(shape):
    zeros = (0,) * len(shape)
    return pl.BlockSpec(shape, lambda *_: zeros)


def _qkv_kernel(x_ref, g_ref, w_ref, qb_ref, k_ref, v_ref, kb_ref, vb_ref):
    d = x_ref.shape[1]
    xn = _rms(x_ref[...], g_ref[...], NORM_EPS).astype(BF16)
    q = _bdot(xn, w_ref[:, 0:d])
    qb_ref[...] = (q * (ATT_HEAD_DIM ** -0.5 * LOG2E)).astype(BF16)
    k = _bdot(xn, w_ref[:, d:2 * d])
    k_ref[...] = k
    kb_ref[...] = k.astype(BF16)
    v = _bdot(xn, w_ref[:, 2 * d:3 * d])
    v_ref[...] = v
    vb_ref[...] = v.astype(BF16)


def _qkv_proj(x2, g, w_bf):
    m, d = x2.shape
    tm = min(TOKEN_TILE, m)
    row = pl.BlockSpec((tm, d), lambda i: (i, 0))
    return pl.pallas_call(
        _qkv_kernel,
        grid=(m // tm,),
        in_specs=[row, _full((1, d)), _full((d, 3 * d))],
        out_specs=[row, row, row, row, row],
        out_shape=[jax.ShapeDtypeStruct((m, d), BF16),
                   jax.ShapeDtypeStruct((m, d), F32),
                   jax.ShapeDtypeStruct((m, d), F32),
                   jax.ShapeDtypeStruct((m, d), BF16),
                   jax.ShapeDtypeStruct((m, d), BF16)],
        compiler_params=_params(1),
        name="qkv_proj",
    )(x2, g.reshape(1, d), w_bf)


def _proj_res_kernel(a_ref, res_ref, w_ref, o_ref):
    o_ref[...] = res_ref[...] + _bdot(a_ref[...].astype(BF16), w_ref[...])


def _proj_gate_res_kernel(a_ref, gate_ref, res_ref, w_ref, o_ref):
    a = (a_ref[...] * gate_ref[...]).astype(BF16)
    o_ref[...] = res_ref[...] + _bdot(a, w_ref[...])


def _proj_res(a2, res2, w_bf, gate2=None):
    m, d = res2.shape
    tm = min(TOKEN_TILE, m)
    row = pl.BlockSpec((tm, d), lambda i: (i, 0))
    if gate2 is None:
        kern, ins, specs = _proj_res_kernel, (a2, res2, w_bf), [row, row, _full((d, d))]
    else:
        kern, ins = _proj_gate_res_kernel, (a2, gate2, res2, w_bf)
        specs = [row, row, row, _full((d, d))]
    return pl.pallas_call(
        kern,
        grid=(m // tm,),
        in_specs=specs,
        out_specs=row,
        out_shape=jax.ShapeDtypeStruct((m, d), F32),
        compiler_params=_params(1),
        name="proj_res",
    )(*ins)


def _ffn_kernel(x_ref, p_ref, gm_ref, w1_ref, w2_ref, gp_ref, win_ref, wg_ref, gf_ref,
                o_ref, *y_ref):
    x = x_ref[...]
    h = _rms(x, gm_ref[...], NORM_EPS).astype(BF16)
    acc = x
    d_ff = w1_ref.shape[1]
    for c in range(d_ff // FF_CHUNK):
        cols = slice(c * FF_CHUNK, (c + 1) * FF_CHUNK)
        a = jnp.maximum(_bdot(h, w1_ref[:, cols]), 0.0)
        acc = acc + _bdot((a * a).astype(BF16), w2_ref[cols, :])
    gate = _sigmoid(_bdot(_rms(acc, gp_ref[...], NORM_EPS).astype(BF16), wg_ref[...]))
    out = acc + _bdot(p_ref[...].astype(BF16), win_ref[...]) * gate
    o_ref[...] = out
    if y_ref:
        y_ref[0][...] = _rms(out, gf_ref[...], NORM_EPS)


def _ffn_ple(x2, p2, g_mlp, w1_bf, w2_bf, g_ple, win_bf, wg_bf, g_final, emit_final):
    m, d = x2.shape
    pd = p2.shape[1]
    d_ff = w1_bf.shape[1]
    tm = min(TOKEN_TILE, m)
    row = pl.BlockSpec((tm, d), lambda i: (i, 0))
    once = dict(pipeline_mode=pl.Buffered(1))
    specs = [row, pl.BlockSpec((tm, pd), lambda i: (i, 0)), _full((1, d)),
             pl.BlockSpec((d, d_ff), lambda i: (0, 0), **once),
             pl.BlockSpec((d_ff, d), lambda i: (0, 0), **once),
             _full((1, d)),
             pl.BlockSpec((pd, d), lambda i: (0, 0), **once),
             pl.BlockSpec((d, d), lambda i: (0, 0), **once),
             _full((1, d))]
    n_out = 2 if emit_final else 1
    out = pl.pallas_call(
        _ffn_kernel,
        grid=(m // tm,),
        in_specs=specs,
        out_specs=[row] * n_out,
        out_shape=[jax.ShapeDtypeStruct((m, d), F32)] * n_out,
        compiler_params=_params(1),
        name="ffn_ple",
    )(x2, p2, g_mlp.reshape(1, d), w1_bf, w2_bf, g_ple.reshape(1, d), win_bf, wg_bf,
      g_final.reshape(1, d))
    return out if emit_final else (out[0], None)


def _bucket_np(n):
    n = np.asarray(n, np.int64)
    nf = np.maximum(n, 1).astype(np.float32)
    large = MAX_EXACT + (np.log(nf / np.float32(MAX_EXACT))
                         / np.float32(math.log(MAX_DISTANCE / MAX_EXACT))
                         * np.float32(N_BUCKETS - MAX_EXACT)).astype(np.int32)
    large = np.minimum(large, N_BUCKETS - 1)
    return np.where(n < MAX_EXACT, n, large).astype(np.int32)


def _bias_kernel(relb_ref, idx_ref, o_ref):
    h = pl.program_id(0)
    idx = idx_ref[...]
    acc = jnp.full(idx.shape, NEG_INF, F32)
    far = relb_ref[N_BUCKETS - 1, h]
    for b in range(N_BUCKETS):
        acc = jnp.where(idx == b, (relb_ref[b, h] - far) * LOG2E, acc)
    o_ref[0] = acc


def _bias_tables(rel_bias, idx):
    r, c = idx.shape
    return pl.pallas_call(
        _bias_kernel,
        grid=(ATT_HEADS,),
        in_specs=[pl.BlockSpec(memory_space=pltpu.SMEM), _full((r, c))],
        out_specs=pl.BlockSpec((1, r, c), lambda h: (h, 0, 0)),
        out_shape=jax.ShapeDtypeStruct((ATT_HEADS, r, c), F32),
        compiler_params=_params(1),
        name="bias_tables",
    )(rel_bias, jnp.asarray(idx.astype(np.int32)))


def _lam_kernel(q1_ref, k1_ref, q2_ref, k2_ref, o_ref, *, lam_init):
    e1 = jnp.exp(jnp.sum(q1_ref[...] * k1_ref[...], axis=-1, keepdims=True))
    e2 = jnp.exp(jnp.sum(q2_ref[...] * k2_ref[...], axis=-1, keepdims=True))
    o_ref[...] = jnp.broadcast_to(e1 - e2 + lam_init, o_ref.shape)


def _diff_lambda(lq1, lk1, lq2, lk2, lam_init):
    n = lq1.shape[0]
    out = pl.pallas_call(
        functools.partial(_lam_kernel, lam_init=lam_init),
        in_specs=[_full((1, n))] * 4,
        out_specs=_full((1, LANES)),
        out_shape=jax.ShapeDtypeStruct((1, LANES), F32),
        grid=(1,),
        compiler_params=_params(1),
        name="diff_lambda",
    )(lq1.reshape(1, n), lk1.reshape(1, n), lq2.reshape(1, n), lk2.reshape(1, n))
    return out[0, :1]


def _online_softmax(s, v1, m_ref, acc_ref, rows):
    tiles = [s[:, c * LANES:(c + 1) * LANES] for c in range(s.shape[1] // LANES)]
    m_prev = m_ref[rows, :]
    row_max = jnp.max(functools.reduce(jnp.maximum, tiles), axis=1, keepdims=True)
    m_new = jnp.maximum(m_prev, row_max)
    alpha = jnp.exp2(m_prev - m_new)
    p = jnp.concatenate([jnp.exp2(t - m_new).astype(BF16) for t in tiles], axis=1)
    acc_ref[rows, :] = jnp.concatenate([alpha, alpha], axis=1) * acc_ref[rows, :] + _bdot(p, v1)
    m_ref[rows, :] = m_new


def _flash_kernel(lam_ref, q_ref, k_ref, v_ref, b0_ref, b1_ref, g_ref, o_ref,
                  qs_ref, v1_ref, m_ref, acc_ref, *, blk, chunk, lam_init):
    i = pl.program_id(2)

    @pl.when(i == 0)
    def _():
        v1_ref[:, 0:ATT_V_DIM] = v_ref[0]
        v1_ref[:, ATT_V_DIM:2 * ATT_V_DIM] = jnp.ones((v1_ref.shape[0], ATT_V_DIM), BF16)

    q = q_ref[0]
    lane = lax.broadcasted_iota(jnp.int32, q.shape, 1)
    zero = jnp.zeros_like(q)
    qs_ref[0:blk, :] = jnp.where(lane < ATT_HEAD_DIM, q, zero)
    qs_ref[blk:2 * blk, :] = jnp.where(lane >= ATT_HEAD_DIM, q, zero)
    m_ref[...] = jnp.full(m_ref.shape, NEG_INF, F32)
    acc_ref[...] = jnp.zeros(acc_ref.shape, F32)

    def step(j, bias):
        cols = pl.ds(pl.multiple_of(j * blk, blk), blk)
        kb = k_ref[0, cols, :]
        v1 = v1_ref[cols, :]
        for c in range(2 * blk // chunk):
            rows = slice(c * chunk, (c + 1) * chunk)
            s = lax.dot_general(qs_ref[rows, :], kb, NT, preferred_element_type=F32)
            if bias is not None:
                r0 = (c * chunk) % blk
                s = s + bias[r0:r0 + chunk, :]
            _online_softmax(s, v1, m_ref, acc_ref, rows)

    n_far = jnp.maximum(i - 1, 0)

    def far_body(jj, carry):
        for u in range(ATT_FAR_UNROLL):
            step(jj * ATT_FAR_UNROLL + u, None)
        return carry

    lax.fori_loop(0, n_far // ATT_FAR_UNROLL, far_body, 0)
    done = (n_far // ATT_FAR_UNROLL) * ATT_FAR_UNROLL
    bit = ATT_FAR_UNROLL // 2
    while bit:
        take = (n_far - done) & bit

        @pl.when(take != 0)
        def _(done=done, bit=bit):
            for u in range(bit):
                step(done + u, None)

        done = done + take
        bit //= 2

    @pl.when(i >= 1)
    def _():
        step(i - 1, b1_ref[0])
        step(i, b0_ref[0])

    @pl.when(i == 0)
    def _():
        step(0, b0_ref[0])

    acc = acc_ref[...]
    o12 = acc[:, 0:ATT_V_DIM] / acc[:, ATT_V_DIM:2 * ATT_V_DIM]
    o = o12[0:blk, :] - lam_ref[0] * o12[blk:2 * blk, :]
    y = _rms(o, g_ref[...], SUBLN_EPS) * (1.0 - lam_init)
    o_ref[0] = y.astype(BF16)


def _flash_bias_tables(rel_bias, t):
    blk = min(ATT_BLOCK, t)
    assert blk >= MAX_DISTANCE and t % blk == 0
    r = np.arange(blk)[:, None]
    c = np.arange(blk)[None, :]
    idx_diag = np.where(r >= c, _bucket_np(np.maximum(r - c, 0)), -1)
    idx_sub = _bucket_np(blk + r - c)
    assert (_bucket_np(np.arange(blk + 1, 4 * blk)) == N_BUCKETS - 1).all()
    return _bias_tables(rel_bias, np.concatenate([idx_diag, idx_sub], axis=0))


def _flash_attn(qb, kb, vb, lam, tabs, subln_g, lam_init):
    b, t, d = qb.shape
    blk = min(ATT_BLOCK, t)
    qspec = pl.BlockSpec((1, blk, ATT_V_DIM), lambda bi, h, i: (bi, i, h))
    kvspec = pl.BlockSpec((1, t, ATT_V_DIM), lambda bi, h, i: (bi, 0, h))
    return pl.pallas_call(
        functools.partial(_flash_kernel, blk=blk, chunk=min(ATT_ROW_CHUNK, blk),
                          lam_init=lam_init),
        grid=(b, ATT_HEADS, t // blk),
        in_specs=[pl.BlockSpec(memory_space=pltpu.SMEM),
                  qspec, kvspec, kvspec,
                  pl.BlockSpec((1, blk, blk), lambda bi, h, i: (h, 0, 0)),
                  pl.BlockSpec((1, blk, blk), lambda bi, h, i: (h, 1, 0)),
                  _full((1, ATT_V_DIM))],
        out_specs=qspec,
        out_shape=jax.ShapeDtypeStruct((b, t, d), BF16),
        scratch_shapes=[pltpu.VMEM((2 * blk, ATT_V_DIM), BF16),
                        pltpu.VMEM((t, 2 * ATT_V_DIM), BF16),
                        pltpu.VMEM((2 * blk, LANES), F32),
                        pltpu.VMEM((2 * blk, 2 * ATT_V_DIM), F32)],
        compiler_params=_params(3),
        name="flash_diff_attn",
    )(lam, qb, kb, vb, tabs, tabs, subln_g.reshape(1, ATT_V_DIM))


def _decode_kernel(pt_ref, lam_ref, q_ref, *refs, tq, lam_init):
    del pt_ref
    ck_refs = refs[0:DECODE_PAGES]
    cv_refs = refs[DECODE_PAGES:2 * DECODE_PAGES]
    (kn_ref, vn_ref, tab_ref, tabn_ref, g_ref, o_ref,
     qa_ref, m_ref, l_ref, acc_ref) = refs[2 * DECODE_PAGES:]
    p = pl.program_id(1)
    n_groups = pl.num_programs(1) - 1
    grp = 2 * tq

    @pl.when(p == 0)
    def _():
        for h in range(ATT_HEADS):
            qh = q_ref[0, :, ATT_V_DIM * h:ATT_V_DIM * (h + 1)].astype(F32)
            lane = lax.broadcasted_iota(jnp.int32, qh.shape, 1)
            qa_ref[h * grp:h * grp + tq, :] = jnp.where(lane < ATT_HEAD_DIM, qh, 0.0)
            qa_ref[h * grp + tq:(h + 1) * grp, :] = jnp.where(lane >= ATT_HEAD_DIM, qh, 0.0)
        m_ref[...] = jnp.full(m_ref.shape, NEG_INF, F32)
        l_ref[...] = jnp.zeros(l_ref.shape, F32)
        acc_ref[...] = jnp.zeros(acc_ref.shape, F32)

    def step(pages):
        qa = qa_ref[...].astype(BF16)
        tiles = []
        for k3, _, tab in pages:
            k2 = k3.reshape(-1, ATT_V_DIM).astype(BF16)
            s = lax.dot_general(qa, k2, NT, preferred_element_type=F32) + tab
            tiles += [s[:, c * LANES:(c + 1) * LANES] for c in range(-(-s.shape[1] // LANES))]
        m_prev = m_ref[...]
        row_max = jnp.max(functools.reduce(jnp.maximum, tiles), axis=1, keepdims=True)
        m_new = jnp.maximum(m_prev, row_max)
        alpha = jnp.exp2(m_prev - m_new)
        ps = [jnp.exp2(t - m_new[:, 0:t.shape[1]]) for t in tiles]
        row_sum = jnp.sum(functools.reduce(jnp.add, ps), axis=1, keepdims=True)
        l_ref[...] = alpha * l_ref[...] + row_sum
        acc = alpha * acc_ref[...]
        per_page = len(ps) // len(pages)
        for n, (_, v3, _) in enumerate(pages):
            v2 = v3.reshape(-1, ATT_V_DIM).astype(BF16)
            pb = jnp.concatenate([x.astype(BF16) for x in ps[n * per_page:(n + 1) * per_page]],
                                 axis=1)
            acc = acc + _bdot(pb, v2)
        acc_ref[...] = acc
        m_ref[...] = m_new

    def cached(last_kind):
        kinds = [0] * (DECODE_PAGES - 1) + [last_kind]
        return [(ck[...], cv[...], tab_ref[kind]) for ck, cv, kind in zip(ck_refs, cv_refs, kinds)]

    @pl.when(p < n_groups - 1)
    def _():
        step(cached(0))

    @pl.when(p == n_groups - 1)
    def _():
        step(cached(1))

    @pl.when(p == n_groups)
    def _():
        step([(kn_ref[0], vn_ref[0], tabn_ref[...])])
        o12 = acc_ref[...] / l_ref[...]
        for h in range(ATT_HEADS):
            o = o12[h * grp:h * grp + tq, :] - lam_ref[0] * o12[h * grp + tq:(h + 1) * grp, :]
            y = _rms(o, g_ref[...], SUBLN_EPS) * (1.0 - lam_init)
            o_ref[0, :, ATT_V_DIM * h:ATT_V_DIM * (h + 1)] = y


def _decode_bias_tables(rel_bias, tq, t_valid):
    tt = np.arange(tq)[:, None]
    cc = np.arange(PAGE_SIZE)[None, :]
    idx_far = np.full((tq, PAGE_SIZE), N_BUCKETS - 1)
    idx_last = _bucket_np(PAGE_SIZE - cc + tt)
    assert (_bucket_np(PAGE_SIZE + 1 + 0 * cc + tt) == N_BUCKETS - 1).all()
    idx_new = np.where((cc <= tt) & (cc < t_valid), _bucket_np(np.maximum(tt - cc, 0)), -1)
    tabs = _bias_tables(rel_bias, np.concatenate([idx_far, idx_last, idx_new], axis=0))
    h = ATT_HEADS
    tb = jnp.transpose(tabs.reshape(h, 3, tq, PAGE_SIZE), (1, 0, 2, 3))
    tb = jnp.broadcast_to(tb[:, :, None, :, :, None], (3, h, 2, tq, PAGE_SIZE, h))
    own = np.eye(h, dtype=bool)[None, :, None, None, None, :]
    tb = jnp.where(own, tb, NEG_INF)
    cached = tb[0:2].reshape(2, h * 2 * tq, PAGE_SIZE * h)
    new = tb[2, :, :, :, 0:tq, :].reshape(h * 2 * tq, tq * h)
    return cached, new


def _decode_attn(qb, k, v, cache_k, cache_v, layer, page_table, lam, tabs, subln_g, lam_init):
    bs, tq, d = qb.shape
    n_pages = page_table.shape[1]
    heads = (ATT_HEADS, ATT_V_DIM)
    kn = k.reshape(bs, tq, *heads)
    vn = v.reshape(bs, tq, *heads)
    tabs, tabs_new = tabs
    rows = ATT_HEADS * 2 * tq
    assert n_pages % DECODE_PAGES == 0
    n_groups = n_pages // DECODE_PAGES

    def page_map(which):
        def index(bi, p, pt):
            return (layer, pt[bi, DECODE_PAGES * jnp.minimum(p, n_groups - 1) + which], 0, 0, 0)
        return index

    seq3 = lambda bi, p, pt: (bi, 0, 0)
    seq4 = lambda bi, p, pt: (bi, 0, 0, 0)
    page = lambda which: pl.BlockSpec((None, None, PAGE_SIZE) + heads, page_map(which))
    new = pl.BlockSpec((1, tq) + heads, seq4)
    grid_spec = pltpu.PrefetchScalarGridSpec(
        num_scalar_prefetch=1,
        grid=(bs, n_groups + 1),
        in_specs=[pl.BlockSpec(memory_space=pltpu.SMEM),
                  pl.BlockSpec((1, tq, d), seq3),
                  *[page(n) for n in range(DECODE_PAGES)] * 2, new, new,
                  pl.BlockSpec(tabs.shape, lambda bi, p, pt: (0, 0, 0)),
                  pl.BlockSpec(tabs_new.shape, lambda bi, p, pt: (0, 0)),
                  pl.BlockSpec((1, ATT_V_DIM), lambda bi, p, pt: (0, 0))],
        out_specs=pl.BlockSpec((1, tq, d), seq3),
        scratch_shapes=[pltpu.VMEM((rows, ATT_V_DIM), F32),
                        pltpu.VMEM((rows, LANES), F32),
                        pltpu.VMEM((rows, LANES), F32),
                        pltpu.VMEM((rows, ATT_V_DIM), F32)])
    return pl.pallas_call(
        functools.partial(_decode_kernel, tq=tq, lam_init=lam_init),
        grid_spec=grid_spec,
        out_shape=jax.ShapeDtypeStruct((bs, tq, d), F32),
        compiler_params=_params(2),
        name="decode_diff_attn",
    )(page_table, lam, qb, *[cache_k] * DECODE_PAGES, *[cache_v] * DECODE_PAGES, kn, vn, tabs,
      tabs_new, subln_g.reshape(1, ATT_V_DIM))


def _rwkv_pre_kernel(*refs, has_vres, t_last):
    if has_vres:
        (x_ref, sh_ref, gm_ref, mu_ref, wrkv_ref, w0_ref, w1_ref, w2_ref, a0_ref, a1_ref, a2_ref,
         g1_ref, g2_ref, vf_ref, v0_ref, v1_ref, v2_ref,
         r_ref, k_ref, v_ref, lw_ref, a_ref, g_ref, last_ref, carry_ref) = refs
    else:
        (x_ref, sh_ref, gm_ref, mu_ref, wrkv_ref, w0_ref, w1_ref, w2_ref, a0_ref, a1_ref, a2_ref,
         g1_ref, g2_ref,
         r_ref, k_ref, v_ref, lw_ref, a_ref, g_ref, last_ref, carry_ref) = refs
    ti = pl.program_id(1)
    tm = x_ref.shape[1]
    xn = _rms(x_ref[0], gm_ref[...], NORM_EPS)

    @pl.when(ti == 0)
    def _():
        carry_ref[...] = sh_ref[0]

    rolled = pltpu.roll(xn, 1, axis=0)
    rid = lax.broadcasted_iota(jnp.int32, xn.shape, 0)
    x_prev = jnp.where(rid == 0, carry_ref[...], rolled)
    carry_ref[...] = xn[tm - 1:tm, :]

    @pl.when(ti == pl.num_programs(1) - 1)
    def _():
        last_ref[0] = xn[t_last:t_last + 1, :]

    xx = x_prev - xn
    mix = lambda n: (xn + xx * mu_ref[n:n + 1, :]).astype(BF16)
    xr, xw, xk, xv, xa, xg = [mix(n) for n in range(6)]
    r_ref[0] = _bdot(xr, wrkv_ref[0])
    k_ref[0] = _bdot(xk, wrkv_ref[1])
    v = _bdot(xv, wrkv_ref[2])
    if has_vres:
        lo = _bdot(_bdot(xv, v1_ref[...]).astype(BF16), v2_ref[...])
        v = v + (vf_ref[0] - v) * _sigmoid(v0_ref[...] + lo)
    v_ref[0] = v
    wl = w0_ref[...] + _bdot(jnp.tanh(_bdot(xw, w1_ref[...])).astype(BF16), w2_ref[...])
    z = -wl
    softplus = jnp.maximum(z, 0.0) + jnp.log(1.0 + jnp.exp(-jnp.abs(z)))
    lw_ref[0] = -jnp.exp(-softplus - 0.5)
    a_ref[0] = _sigmoid(a0_ref[...] + _bdot(_bdot(xa, a1_ref[...]).astype(BF16), a2_ref[...]))
    g_ref[0] = _bdot(_sigmoid(_bdot(xg, g1_ref[...])).astype(BF16), g2_ref[...])


def _pad_lora(w_in, w_out):
    n = w_in.shape[1]
    n_pad = -(-n // LANES) * LANES - n
    return (jnp.pad(w_in, ((0, 0), (0, n_pad))).astype(BF16),
            jnp.pad(w_out, ((0, n_pad), (0, 0))).astype(BF16))


def _rwkv_pre(x3, shift, g_mix, mu, wrkv_bf, w0, w1, w2, a0, a1, a2, g1, g2, vres, v_first, t_valid):
    nseq, t, d = x3.shape
    tm = min(RWKV_TILE, t)
    nt = t // tm
    t_last = (t_valid - 1) - (nt - 1) * tm
    tile = pl.BlockSpec((1, tm, d), lambda s, i: (s, i, 0))
    vec = _full((1, d))
    w1p, w2p = _pad_lora(w1, w2)
    a1p, a2p = _pad_lora(a1, a2)
    g1p, g2p = _pad_lora(g1, g2)
    ins = [x3, shift.reshape(nseq, 1, d), g_mix.reshape(1, d), mu, wrkv_bf,
           w0.reshape(1, d), w1p, w2p, a0.reshape(1, d), a1p, a2p, g1p, g2p]
    specs = [tile, pl.BlockSpec((1, 1, d), lambda s, i: (s, 0, 0)), vec, _full(mu.shape),
             pl.BlockSpec(wrkv_bf.shape, lambda s, i: (0, 0, 0), pipeline_mode=pl.Buffered(1)),
             vec, _full(w1p.shape), _full(w2p.shape),
             vec, _full(a1p.shape), _full(a2p.shape), _full(g1p.shape), _full(g2p.shape)]
    if vres is not None:
        v0, v1, v2 = vres
        v1p, v2p = _pad_lora(v1, v2)
        ins += [v_first, v0.reshape(1, d), v1p, v2p]
        specs += [tile, vec, _full(v1p.shape), _full(v2p.shape)]
    big = jax.ShapeDtypeStruct((nseq, t, d), F32)
    outs = pl.pallas_call(
        functools.partial(_rwkv_pre_kernel, has_vres=vres is not None, t_last=t_last),
        grid=(nseq, nt),
        in_specs=specs,
        out_specs=[tile] * 6 + [pl.BlockSpec((1, 1, d), lambda s, i: (s, 0, 0))],
        out_shape=[big] * 6 + [jax.ShapeDtypeStruct((nseq, 1, d), F32)],
        scratch_shapes=[pltpu.VMEM((1, d), F32)],
        compiler_params=_params(2),
        name="rwkv_pre",
    )(*ins)
    return outs


def _lane_half(shape, period=None):
    period = period or shape[1]
    lane = lax.broadcasted_iota(jnp.int32, shape, 1)
    return (lane % period) < period // 2


def _bd_rows(x, period=None):
    first = _lane_half(x.shape, period)
    zero = jnp.zeros_like(x)
    return jnp.concatenate([jnp.where(first, x, zero), jnp.where(first, zero, x)], axis=0)


def _bd_split(x, period=None):
    return tuple(_bd_rows(part, period) for part in _split(x))


def _pair_diag(full):
    n = full.shape[0] // 2
    return jnp.where(_lane_half((n, 2 * n)), full[0:n, :], full[n:2 * n, :])


def _pair_sum(x):
    first = _lane_half(x.shape)
    lo = jnp.sum(jnp.where(first, x, 0.0), axis=-1, keepdims=True)
    hi = jnp.sum(jnp.where(first, 0.0, x), axis=-1, keepdims=True)
    return jnp.where(first, lo, hi)


def _wkv_prep_kernel(r_ref, k_ref, v_ref, lw_ref, a_ref, kk_ref, ka_ref, rk_ref,
                     rp_ref, yc_ref, bn_ref, g_ref, h_ref, *, t_valid):
    ln = r_ref.shape[1]
    d = r_ref.shape[2]
    n = RW_HEAD
    pairs = range(d // (2 * n))
    sl = [slice(p * 2 * n, (p + 1) * 2 * n) for p in pairs]
    ri = lax.broadcasted_iota(jnp.int32, (ln, ln), 0)
    ci = lax.broadcasted_iota(jnp.int32, (ln, ln), 1)
    incl = (ri >= ci).astype(F32)
    incl2 = jnp.concatenate([incl, incl], axis=1)
    strict2 = jnp.concatenate([(ri > ci).astype(F32)] * 2, axis=1)
    eye_l2 = jnp.concatenate([(ri == ci).astype(F32)] * 2, axis=1)
    eye_n = (lax.broadcasted_iota(jnp.int32, (n, n), 0)
             == lax.broadcasted_iota(jnp.int32, (n, n), 1)).astype(F32)
    r, k, v, lw, a = r_ref[0], k_ref[0], v_ref[0], lw_ref[0], a_ref[0]
    if t_valid is not None:
        t_glob = pl.program_id(1) * ln + lax.broadcasted_iota(jnp.int32, (ln, 1), 0)
        live = (t_glob < t_valid).astype(F32)
        r, k, v, lw, a = r * live, k * live, v * live, lw * live, a * live

    cum = jnp.dot(incl, lw, precision=HIGHEST, preferred_element_type=F32)
    cum_last = cum[ln - 1:ln, :]
    e_neg = jnp.exp(-cum)
    e_out = jnp.exp(cum_last - cum)
    w_last = jnp.exp(cum_last)
    rt = r * jnp.exp(cum)
    kmod = k * (1.0 + (a - 1.0) * ka_ref[...])
    kt = kmod * e_neg
    ko = kmod * e_out
    kk0 = k * kk_ref[...]
    rkr = r * kmod * rk_ref[...]
    kk = jnp.concatenate(
        [kk0[:, sl[p]] / jnp.maximum(jnp.sqrt(_pair_sum(kk0[:, sl[p]] * kk0[:, sl[p]])), 1e-12)
         for p in pairs], axis=1)
    bn_ref[0] = jnp.concatenate([_pair_sum(rkr[:, sl[p]]) for p in pairs], axis=1) * v
    b_ = kk * a
    at_all = -kk * jnp.exp(cum - lw)
    bt_all = b_ * e_neg
    bo_all = b_ * e_out

    at = [at_all[:, sl[p]] for p in pairs]
    bo = [bo_all[:, sl[p]] for p in pairs]
    vp = [v[:, sl[p]] for p in pairs]
    cat0 = lambda u, w: tuple(jnp.concatenate([x, y], axis=0) for x, y in zip(u, w))
    cat1 = lambda u, w: tuple(jnp.concatenate([x, y], axis=1) for x, y in zip(u, w))
    vbd = [_bd_split(vp[p]) for p in pairs]
    sc = [_dot3p(_split(jnp.concatenate([at[p], rt[:, sl[p]]], axis=0)),
                 cat0(_bd_split(bt_all[:, sl[p]]), _bd_split(kt[:, sl[p]])), NT)
          for p in pairs]
    ab = [sc[p][0:ln, 0:2 * ln] * strict2 for p in pairs]
    ak = [sc[p][0:ln, 2 * ln:4 * ln] * strict2 for p in pairs]
    rb = [sc[p][ln:2 * ln, 0:2 * ln] * incl2 for p in pairs]
    rk = [sc[p][ln:2 * ln, 2 * ln:4 * ln] * incl2 for p in pairs]
    tinv = [eye_l2 + ab[p] for p in pairs]
    pw = ab
    pw_s = [_split(pw[p]) for p in pairs]
    for _ in range(int(math.log2(ln)) - 1):
        pw = [_dot3p(pw_s[p], tuple(_bd_rows(x) for x in pw_s[p])) for p in pairs]
        pw_s = [_split(pw[p]) for p in pairs]
        tinv = [tinv[p] + _dot3p(pw_s[p], _bd_split(tinv[p])) for p in pairs]
    akv = [_dot3p(_split(ak[p]), vbd[p]) for p in pairs]
    x = [_dot3p(_split(tinv[p]), cat1(_bd_split(at[p]), _bd_split(akv[p])))
         for p in pairs]
    zero = (jnp.zeros((2 * ln, 2 * n), BF16),) * 2
    y = [_dot3p(_split(jnp.concatenate([rb[p], rk[p]], axis=1)),
                cat0(_bd_split(x[p], 2 * n), cat1(zero, vbd[p])))
         for p in pairs]
    rp_ref[0] = rt + jnp.concatenate([y[p][:, 0:2 * n] for p in pairs], axis=1)
    yc_ref[0] = jnp.concatenate([y[p][:, 2 * n:4 * n] for p in pairs], axis=1)
    gm = [_pair_diag(_dot3(bo[p], x[p][:, 0:2 * n], TN)) for p in pairs]
    hm = [_pair_diag(_dot3(jnp.concatenate([bo[p], ko[:, sl[p]]], axis=0),
                           jnp.concatenate([x[p][:, 2 * n:4 * n], vp[p]], axis=0), TN))
          for p in pairs]
    g_ref[0, 0] = jnp.concatenate([eye_n] * (d // n), axis=1) * w_last + jnp.concatenate(gm, axis=1)
    h_ref[0, 0] = jnp.concatenate(hm, axis=1)


def _wkv_scan_kernel(rp_ref, yc_ref, bn_ref, g_ref, h_ref, s0_ref, lnw_ref, lnb_ref,
                     z_ref, sf_ref, s_ref):
    n = RW_HEAD
    d = rp_ref.shape[2]
    pairs = range(d // (2 * n))
    sl = [slice(p * 2 * n, (p + 1) * 2 * n) for p in pairs]

    @pl.when(pl.program_id(1) == 0)
    def _():
        s_ref[...] = s0_ref[0]

    s0 = [_bd_split(s_ref[:, sl[p]]) for p in pairs]
    y = [_dot3p(_split(rp_ref[0, :, sl[p]]), s0[p]) + yc_ref[0, :, sl[p]] for p in pairs]
    s1 = [_dot3p(_split(g_ref[0, 0, :, sl[p]]), s0[p]) + h_ref[0, 0, :, sl[p]] for p in pairs]
    s_new = jnp.concatenate(s1, axis=1)
    s_ref[...] = s_new
    sf_ref[0] = s_new
    yc = [y[p] - _pair_sum(y[p]) * (1.0 / n) for p in pairs]
    var = [_pair_sum(yc[p] * yc[p]) * (1.0 / n) for p in pairs]
    yn = jnp.concatenate([yc[p] * lax.rsqrt(var[p] + GN_EPS) for p in pairs], axis=1)
    z_ref[0] = yn * lnw_ref[...] + lnb_ref[...] + bn_ref[0]


def _wkv(r, k, v, lw, a, s0, k_k, k_a, r_k, ln_w, ln_b, t_valid):
    nseq, t, d = r.shape
    n = RW_HEAD
    ln = min(WKV_CHUNK, t)
    nc = t // ln
    tile = pl.BlockSpec((1, ln, d), lambda s, c: (s, c, 0))
    vec = _full((1, d))
    mat = pl.BlockSpec((1, 1, n, d), lambda s, c: (s, c, 0, 0))
    big = jax.ShapeDtypeStruct((nseq, t, d), F32)
    mats = jax.ShapeDtypeStruct((nseq, nc, n, d), F32)
    rp, yc, bn, g, hm = pl.pallas_call(
        functools.partial(_wkv_prep_kernel, t_valid=None if t_valid == t else t_valid),
        grid=(nseq, nc),
        in_specs=[tile] * 5 + [vec] * 3,
        out_specs=[tile] * 3 + [mat] * 2,
        out_shape=[big] * 3 + [mats] * 2,
        compiler_params=_params(2),
        name="wkv_prep",
    )(r, k, v, lw, a, k_k.reshape(1, d), k_a.reshape(1, d), r_k.reshape(1, d))
    s0_l = jnp.transpose(s0, (0, 3, 1, 2)).reshape(nseq, n, d)
    state = pl.BlockSpec((1, n, d), lambda s, c: (s, 0, 0))
    z, sf = pl.pallas_call(
        _wkv_scan_kernel,
        grid=(nseq, nc),
        in_specs=[tile] * 3 + [mat] * 2 + [state, vec, vec],
        out_specs=[tile, state],
        out_shape=[big, jax.ShapeDtypeStruct((nseq, n, d), F32)],
        scratch_shapes=[pltpu.VMEM((n, d), F32)],
        compiler_params=_params(2),
        name="wkv_scan",
    )(rp, yc, bn, g, hm, s0_l, ln_w.reshape(1, d), ln_b.reshape(1, d))
    return z, jnp.transpose(sf.reshape(nseq, n, d // n, n), (0, 2, 3, 1))


def kernel(x_prompt, x_sample, p_prompt, p_sample, cache_k, cache_v, state_wkv, state_shift, page_table, rel_bias, norm_mix, norm_mlp, norm_ple, norm_final, att_w_qkv, att_w_o, att_lam_q1, att_lam_k1, att_lam_q2, att_lam_k2, att_subln, rw_mu, rw_w_rkv, rw_w0, rw_w1, rw_w2, rw_a0, rw_a1, rw_a2, rw_v0, rw_v1, rw_v2, rw_g1, rw_g2, rw_k_k, rw_k_a, rw_r_k, rw_ln_w, rw_ln_b, rw_w_o, mlp_w1, mlp_w2, ple_w_in, ple_w_gate):
    depth = norm_mix.shape[0]
    bp, tp, d = x_prompt.shape
    bs, ts, _ = x_sample.shape
    ts_pad = -(-ts // SUBLANES) * SUBLANES
    heads_rw = d // RW_HEAD
    page_table = page_table.astype(jnp.int32)

    xp = x_prompt.reshape(bp * tp, d)
    xs = jnp.pad(x_sample, ((0, 0), (0, ts_pad - ts), (0, 0))).reshape(bs * ts_pad, d)
    pp_all = p_prompt.reshape(depth, bp * tp, -1)
    ps_all = jnp.pad(p_sample, ((0, 0), (0, 0), (0, ts_pad - ts), (0, 0))).reshape(depth, bs * ts_pad, -1)
    flash_tabs = _flash_bias_tables(rel_bias, tp)
    decode_tabs = _decode_bias_tables(rel_bias, ts_pad, ts)

    kp_l, vp_l, ks_l, vs_l = [], [], [], []
    wkvp_l, shp_l, wkvs_l, shs_l = [], [], [], []
    vf_p = vf_s = None
    y_p = y_s = None
    for i in range(depth):
        if i % 2 == 0:
            a = i // 2
            lam_init = 0.8 - 0.6 * math.exp(-0.3 * i)
            lam = _diff_lambda(att_lam_q1[a], att_lam_k1[a], att_lam_q2[a], att_lam_k2[a], lam_init)
            wqkv = att_w_qkv[a].astype(BF16)
            wo = att_w_o[a].astype(BF16)
            qb, k, v, kb, vb = _qkv_proj(xp, norm_mix[i], wqkv)
            o = _flash_attn(qb.reshape(bp, tp, d), kb.reshape(bp, tp, d), vb.reshape(bp, tp, d),
                            lam, flash_tabs, att_subln[a], lam_init)
            xp = _proj_res(o.reshape(bp * tp, d), xp, wo)
            kp_l.append(k.reshape(bp, tp, ATT_HEADS, ATT_V_DIM))
            vp_l.append(v.reshape(bp, tp, ATT_HEADS, ATT_V_DIM))

            qb, k, v, _, _ = _qkv_proj(xs, norm_mix[i], wqkv)
            k = k.reshape(bs, ts_pad, d)
            v = v.reshape(bs, ts_pad, d)
            o = _decode_attn(qb.reshape(bs, ts_pad, d), k, v, cache_k, cache_v, a, page_table, lam,
                             decode_tabs, att_subln[a], lam_init)
            xs = _proj_res(o.reshape(bs * ts_pad, d), xs, wo)
            ks_l.append(k.reshape(bs, ts_pad, ATT_HEADS, ATT_V_DIM)[:, :ts])
            vs_l.append(v.reshape(bs, ts_pad, ATT_HEADS, ATT_V_DIM)[:, :ts])
        else:
            r = i // 2
            vres = None if r == 0 else (rw_v0[r - 1], rw_v1[r - 1], rw_v2[r - 1])
            wrkv = rw_w_rkv[r].astype(BF16)
            wo = rw_w_o[r].astype(BF16)
            lora = (rw_w0[r], rw_w1[r], rw_w2[r], rw_a0[r], rw_a1[r], rw_a2[r], rw_g1[r], rw_g2[r])
            heads = (rw_k_k[r], rw_k_a[r], rw_r_k[r].reshape(d), rw_ln_w[r], rw_ln_b[r])

            def time_mix(x2, nseq, t, shift, s0, v_first, t_valid):
                rr, kk, vv, lw, aa, gg, last = _rwkv_pre(
                    x2.reshape(nseq, t, d), shift, norm_mix[i], rw_mu[r], wrkv, *lora, vres,
                    v_first, t_valid)
                z, s_fin = _wkv(rr, kk, vv, lw, aa, s0, *heads, t_valid)
                out = _proj_res(z.reshape(nseq * t, d), x2, wo, gate2=gg.reshape(nseq * t, d))
                return out, s_fin, last.reshape(nseq, d), (vv if vres is None else v_first)

            xp, s_p, last_p, vf_p = time_mix(
                xp, bp, tp, jnp.zeros((bp, d), F32),
                jnp.zeros((bp, heads_rw, RW_HEAD, RW_HEAD), F32), vf_p, tp)
            xs, s_s, last_s, vf_s = time_mix(xs, bs, ts_pad, state_shift[r], state_wkv[r], vf_s, ts)
            wkvp_l.append(s_p)
            shp_l.append(last_p)
            wkvs_l.append(s_s)
            shs_l.append(last_s)
        final = i == depth - 1
        ffn_w = (norm_mlp[i], mlp_w1[i].astype(BF16), mlp_w2[i].astype(BF16), norm_ple[i],
                 ple_w_in[i].astype(BF16), ple_w_gate[i].astype(BF16), norm_final)
        xp, y_p = _ffn_ple(xp, pp_all[i], *ffn_w, final)
        xs, y_s = _ffn_ple(xs, ps_all[i], *ffn_w, final)

    y_prompt = y_p.reshape(bp, tp, d)
    y_sample = y_s.reshape(bs, ts_pad, d)[:, :ts]
    return (y_prompt, y_sample, jnp.stack(kp_l), jnp.stack(vp_l), jnp.stack(wkvp_l),
            jnp.stack(shp_l), jnp.stack(ks_l), jnp.stack(vs_l), jnp.stack(wkvs_l),
            jnp.stack(shs_l))
```

```python
import functools
import math

import numpy as np
import jax
import jax.numpy as jnp
from jax import lax
from jax.experimental import pallas as pl
from jax.experimental.pallas import tpu as pltpu

F32 = jnp.float32
BF16 = jnp.bfloat16
HIGHEST = lax.Precision.HIGHEST

ATT_HEADS = 8
ATT_HEAD_DIM = 64
ATT_V_DIM = 128
N_BUCKETS = 32
MAX_EXACT = 16
MAX_DISTANCE = 128
PAGE_SIZE = 128
RW_HEAD = 64
NORM_EPS = 1e-6
SUBLN_EPS = 1e-5
GN_EPS = 64e-5
NEG_INF = -1e30
LOG2E = math.log2(math.e)

LANES = 128
SUBLANES = 8
VMEM_LIMIT = 56 * 1024 * 1024
TOKEN_TILE = 512
RWKV_TILE = 256
ATT_BLOCK = 512
ATT_FAR_UNROLL = 8
ATT_ROW_CHUNK = 128
DECODE_PAGES = 8
WKV_CHUNK = 64
FF_CHUNK = 1024

NN = (((1,), (0,)), ((), ()))
NT = (((1,), (1,)), ((), ()))
TN = (((0,), (0,)), ((), ()))


def _params(n_axes):
    return pltpu.CompilerParams(dimension_semantics=("arbitrary",) * n_axes,
                                vmem_limit_bytes=VMEM_LIMIT)


def _rms(x, g, eps):
    return x * lax.rsqrt(jnp.mean(x * x, axis=-1, keepdims=True) + eps) * g


def _sigmoid(x):
    return 1.0 / (1.0 + jnp.exp(-x))


def _bdot(a, b):
    return jnp.dot(a, b, preferred_element_type=F32)


def _split(a):
    hi = a.astype(BF16)
    return hi, (a - hi.astype(F32)).astype(BF16)


def _dot3(a, b, dims=NN):
    ah, al = _split(a)
    bh, bl = _split(b)
    d = lambda x, y: lax.dot_general(x, y, dims, preferred_element_type=F32)
    return d(ah, bh) + (d(ah, bl) + d(al, bh))


def _dot3p(a, b, dims=NN):
    d = lambda x, y: lax.dot_general(x, y, dims, preferred_element_type=F32)
    return d(a[0], b[0]) + (d(a[0], b[1]) + d(a[1], b[0]))


def _full(shape):
    zeros = (0,) * len(shape)
    return pl.BlockSpec(shape, lambda *_: zeros)


def _qkv_kernel(x_ref, g_ref, w_ref, *refs):
    qb_ref, k_ref, v_ref, kb_ref, vb_ref = refs[-5:]
    d = x_ref.shape[1]
    xn = _rms(x_ref[...], g_ref[...], NORM_EPS).astype(BF16)
    q = _bdot(xn, w_ref[:, 0:d])
    qb_ref[...] = (q * (ATT_HEAD_DIM ** -0.5 * LOG2E)).astype(BF16)
    k = _bdot(xn, w_ref[:, d:2 * d])
    k_ref[...] = k
    kb_ref[...] = k.astype(BF16)
    v = _bdot(xn, w_ref[:, 2 * d:3 * d])
    v_ref[...] = v
    vb_ref[...] = v.astype(BF16)


def _qkv_proj(x2, g, w_bf, layer=0, n_layers=1, stacked=None):
    m, d = x2.shape
    tm = min(TOKEN_TILE, m)
    row = pl.BlockSpec((tm, d), lambda i: (i, 0))
    slab = pl.BlockSpec((None, tm, d), lambda i: (layer, i, 0))
    ins = [x2, g.reshape(1, d), w_bf]
    specs = [row, _full((1, d)), _full((d, 3 * d))]
    aliases = {}
    if stacked is not None:
        ins += list(stacked)
        specs += [pl.BlockSpec(memory_space=pl.ANY)] * 2
        aliases = {3: 1, 4: 2}
    return pl.pallas_call(
        _qkv_kernel,
        grid=(m // tm,),
        in_specs=specs,
        out_specs=[row, slab, slab, row, row],
        out_shape=[jax.ShapeDtypeStruct((m, d), BF16),
                   jax.ShapeDtypeStruct((n_layers, m, d), F32),
                   jax.ShapeDtypeStruct((n_layers, m, d), F32),
                   jax.ShapeDtypeStruct((m, d), BF16),
                   jax.ShapeDtypeStruct((m, d), BF16)],
        input_output_aliases=aliases,
        compiler_params=_params(1),
        name="qkv_proj",
    )(*ins)


def _proj_res_kernel(a_ref, res_ref, w_ref, o_ref):
    o_ref[...] = res_ref[...] + _bdot(a_ref[...].astype(BF16), w_ref[...])


def _proj_gate_res_kernel(a_ref, gate_ref, res_ref, w_ref, o_ref):
    a = (a_ref[...] * gate_ref[...]).astype(BF16)
    o_ref[...] = res_ref[...] + _bdot(a, w_ref[...])


def _proj_res(a2, res2, w_bf, gate2=None):
    m, d = res2.shape
    tm = min(TOKEN_TILE, m)
    row = pl.BlockSpec((tm, d), lambda i: (i, 0))
    if gate2 is None:
        kern, ins, specs = _proj_res_kernel, (a2, res2, w_bf), [row, row, _full((d, d))]
    else:
        kern, ins = _proj_gate_res_kernel, (a2, gate2, res2, w_bf)
        specs = [row, row, row, _full((d, d))]
    return pl.pallas_call(
        kern,
        grid=(m // tm,),
        in_specs=specs,
        out_specs=row,
        out_shape=jax.ShapeDtypeStruct((m, d), F32),
        compiler_params=_params(1),
        name="proj_res",
    )(*ins)


def _ffn_kernel(x_ref, p_ref, gm_ref, w1_ref, w2_ref, gp_ref, win_ref, wg_ref, gf_ref,
                o_ref, *y_ref):
    x = x_ref[...]
    h = _rms(x, gm_ref[...], NORM_EPS).astype(BF16)
    acc = x
    d_ff = w1_ref.shape[1]
    for c in range(d_ff // FF_CHUNK):
        cols = slice(c * FF_CHUNK, (c + 1) * FF_CHUNK)
        a = jnp.maximum(_bdot(h, w1_ref[:, cols]), 0.0)
        acc = acc + _bdot((a * a).astype(BF16), w2_ref[cols, :])
    gate = _sigmoid(_bdot(_rms(acc, gp_ref[...], NORM_EPS).astype(BF16), wg_ref[...]))
    out = acc + _bdot(p_ref[...].astype(BF16), win_ref[...]) * gate
    o_ref[...] = out
    if y_ref:
        y_ref[0][...] = _rms(out, gf_ref[...], NORM_EPS)


def _ffn_ple(x2, p2, g_mlp, w1_bf, w2_bf, g_ple, win_bf, wg_bf, g_final, emit_final):
    m, d = x2.shape
    pd = p2.shape[1]
    d_ff = w1_bf.shape[1]
    tm = min(TOKEN_TILE, m)
    row = pl.BlockSpec((tm, d), lambda i: (i, 0))
    once = dict(pipeline_mode=pl.Buffered(1))
    specs = [row, pl.BlockSpec((tm, pd), lambda i: (i, 0)), _full((1, d)),
             pl.BlockSpec((d, d_ff), lambda i: (0, 0), **once),
             pl.BlockSpec((d_ff, d), lambda i: (0, 0), **once),
             _full((1, d)),
             pl.BlockSpec((pd, d), lambda i: (0, 0), **once),
             pl.BlockSpec((d, d), lambda i: (0, 0), **once),
             _full((1, d))]
    n_out = 2 if emit_final else 1
    out = pl.pallas_call(
        _ffn_kernel,
        grid=(m // tm,),
        in_specs=specs,
        out_specs=[row] * n_out,
        out_shape=[jax.ShapeDtypeStruct((m, d), F32)] * n_out,
        compiler_params=_params(1),
        name="ffn_ple",
    )(x2, p2, g_mlp.reshape(1, d), w1_bf, w2_bf, g_ple.reshape(1, d), win_bf, wg_bf,
      g_final.reshape(1, d))
    return out if emit_final else (out[0], None)


def _bucket_np(n):
    n = np.asarray(n, np.int64)
    nf = np.maximum(n, 1).astype(np.float32)
    large = MAX_EXACT + (np.log(nf / np.float32(MAX_EXACT))
                         / np.float32(math.log(MAX_DISTANCE / MAX_EXACT))
                         * np.float32(N_BUCKETS - MAX_EXACT)).astype(np.int32)
    large = np.minimum(large, N_BUCKETS - 1)
    return np.where(n < MAX_EXACT, n, large).astype(np.int32)


def _bias_kernel(relb_ref, idx_ref, o_ref):
    h = pl.program_id(0)
    idx = idx_ref[...]
    acc = jnp.full(idx.shape, NEG_INF, F32)
    far = relb_ref[N_BUCKETS - 1, h]
    for b in range(N_BUCKETS):
        acc = jnp.where(idx == b, (relb_ref[b, h] - far) * LOG2E, acc)
    o_ref[0] = acc


def _bias_tables(rel_bias, idx):
    r, c = idx.shape
    return pl.pallas_call(
        _bias_kernel,
        grid=(ATT_HEADS,),
        in_specs=[pl.BlockSpec(memory_space=pltpu.SMEM), _full((r, c))],
        out_specs=pl.BlockSpec((1, r, c), lambda h: (h, 0, 0)),
        out_shape=jax.ShapeDtypeStruct((ATT_HEADS, r, c), F32),
        compiler_params=_params(1),
        name="bias_tables",
    )(rel_bias, jnp.asarray(idx.astype(np.int32)))


def _lam_kernel(q1_ref, k1_ref, q2_ref, k2_ref, o_ref, *, lam_init):
    e1 = jnp.exp(jnp.sum(q1_ref[...] * k1_ref[...], axis=-1, keepdims=True))
    e2 = jnp.exp(jnp.sum(q2_ref[...] * k2_ref[...], axis=-1, keepdims=True))
    o_ref[...] = jnp.broadcast_to(e1 - e2 + lam_init, o_ref.shape)


def _diff_lambda(lq1, lk1, lq2, lk2, lam_init):
    n = lq1.shape[0]
    out = pl.pallas_call(
        functools.partial(_lam_kernel, lam_init=lam_init),
        in_specs=[_full((1, n))] * 4,
        out_specs=_full((1, LANES)),
        out_shape=jax.ShapeDtypeStruct((1, LANES), F32),
        grid=(1,),
        compiler_params=_params(1),
        name="diff_lambda",
    )(lq1.reshape(1, n), lk1.reshape(1, n), lq2.reshape(1, n), lk2.reshape(1, n))
    return out[0, :1]


def _online_softmax(s, v1, m_ref, acc_ref, rows):
    tiles = [s[:, c * LANES:(c + 1) * LANES] for c in range(s.shape[1] // LANES)]
    m_prev = m_ref[rows, :]
    row_max = jnp.max(functools.reduce(jnp.maximum, tiles), axis=1, keepdims=True)
    m_new = jnp.maximum(m_prev, row_max)
    alpha = jnp.exp2(m_prev - m_new)
    p = jnp.concatenate([jnp.exp2(t - m_new).astype(BF16) for t in tiles], axis=1)
    acc_ref[rows, :] = jnp.concatenate([alpha, alpha], axis=1) * acc_ref[rows, :] + _bdot(p, v1)
    m_ref[rows, :] = m_new


def _flash_kernel(lam_ref, q_ref, k_ref, v_ref, b0_ref, b1_ref, g_ref, o_ref,
                  qs_ref, v1_ref, m_ref, acc_ref, *, blk, chunk, lam_init):
    i = pl.program_id(2)

    @pl.when(i == 0)
    def _():
        v1_ref[:, 0:ATT_V_DIM] = v_ref[0]
        v1_ref[:, ATT_V_DIM:2 * ATT_V_DIM] = jnp.ones((v1_ref.shape[0], ATT_V_DIM), BF16)

    q = q_ref[0]
    lane = lax.broadcasted_iota(jnp.int32, q.shape, 1)
    zero = jnp.zeros_like(q)
    qs_ref[0:blk, :] = jnp.where(lane < ATT_HEAD_DIM, q, zero)
    qs_ref[blk:2 * blk, :] = jnp.where(lane >= ATT_HEAD_DIM, q, zero)
    m_ref[...] = jnp.full(m_ref.shape, NEG_INF, F32)
    acc_ref[...] = jnp.zeros(acc_ref.shape, F32)

    def step(j, bias):
        cols = pl.ds(pl.multiple_of(j * blk, blk), blk)
        kb = k_ref[0, cols, :]
        v1 = v1_ref[cols, :]
        for c in range(2 * blk // chunk):
            rows = slice(c * chunk, (c + 1) * chunk)
            s = lax.dot_general(qs_ref[rows, :], kb, NT, preferred_element_type=F32)
            if bias is not None:
                r0 = (c * chunk) % blk
                s = s + bias[r0:r0 + chunk, :]
            _online_softmax(s, v1, m_ref, acc_ref, rows)

    n_far = jnp.maximum(i - 1, 0)

    def far_body(jj, carry):
        for u in range(ATT_FAR_UNROLL):
            step(jj * ATT_FAR_UNROLL + u, None)
        return carry

    lax.fori_loop(0, n_far // ATT_FAR_UNROLL, far_body, 0)
    done = (n_far // ATT_FAR_UNROLL) * ATT_FAR_UNROLL
    bit = ATT_FAR_UNROLL // 2
    while bit:
        take = (n_far - done) & bit

        @pl.when(take != 0)
        def _(done=done, bit=bit):
            for u in range(bit):
                step(done + u, None)

        done = done + take
        bit //= 2

    @pl.when(i >= 1)
    def _():
        step(i - 1, b1_ref[0])
        step(i, b0_ref[0])

    @pl.when(i == 0)
    def _():
        step(0, b0_ref[0])

    acc = acc_ref[...]
    o12 = acc[:, 0:ATT_V_DIM] / acc[:, ATT_V_DIM:2 * ATT_V_DIM]
    o = o12[0:blk, :] - lam_ref[0] * o12[blk:2 * blk, :]
    y = _rms(o, g_ref[...], SUBLN_EPS) * (1.0 - lam_init)
    o_ref[0] = y.astype(BF16)


def _flash_bias_tables(rel_bias, t):
    blk = min(ATT_BLOCK, t)
    assert blk >= MAX_DISTANCE and t % blk == 0
    r = np.arange(blk)[:, None]
    c = np.arange(blk)[None, :]
    idx_diag = np.where(r >= c, _bucket_np(np.maximum(r - c, 0)), -1)
    idx_sub = _bucket_np(blk + r - c)
    assert (_bucket_np(np.arange(blk + 1, 4 * blk)) == N_BUCKETS - 1).all()
    return _bias_tables(rel_bias, np.concatenate([idx_diag, idx_sub], axis=0))


def _flash_attn(qb, kb, vb, lam, tabs, subln_g, lam_init):
    b, t, d = qb.shape
    blk = min(ATT_BLOCK, t)
    qspec = pl.BlockSpec((1, blk, ATT_V_DIM), lambda bi, h, i: (bi, i, h))
    kvspec = pl.BlockSpec((1, t, ATT_V_DIM), lambda bi, h, i: (bi, 0, h))
    return pl.pallas_call(
        functools.partial(_flash_kernel, blk=blk, chunk=min(ATT_ROW_CHUNK, blk),
                          lam_init=lam_init),
        grid=(b, ATT_HEADS, t // blk),
        in_specs=[pl.BlockSpec(memory_space=pltpu.SMEM),
                  qspec, kvspec, kvspec,
                  pl.BlockSpec((1, blk, blk), lambda bi, h, i: (h, 0, 0)),
                  pl.BlockSpec((1, blk, blk), lambda bi, h, i: (h, 1, 0)),
                  _full((1, ATT_V_DIM))],
        out_specs=qspec,
        out_shape=jax.ShapeDtypeStruct((b, t, d), BF16),
        scratch_shapes=[pltpu.VMEM((2 * blk, ATT_V_DIM), BF16),
                        pltpu.VMEM((t, 2 * ATT_V_DIM), BF16),
                        pltpu.VMEM((2 * blk, LANES), F32),
                        pltpu.VMEM((2 * blk, 2 * ATT_V_DIM), F32)],
        compiler_params=_params(3),
        name="flash_diff_attn",
    )(lam, qb, kb, vb, tabs, tabs, subln_g.reshape(1, ATT_V_DIM))


def _decode_kernel(pt_ref, lam_ref, q_ref, *refs, tq, lam_init):
    del pt_ref
    ck_refs = refs[0:DECODE_PAGES]
    cv_refs = refs[DECODE_PAGES:2 * DECODE_PAGES]
    (kn_ref, vn_ref, tab_ref, tabn_ref, g_ref, o_ref,
     qa_ref, m_ref, l_ref, acc_ref) = refs[2 * DECODE_PAGES:]
    p = pl.program_id(1)
    n_groups = pl.num_programs(1) - 1
    grp = 2 * tq

    @pl.when(p == 0)
    def _():
        for h in range(ATT_HEADS):
            qh = q_ref[0, :, ATT_V_DIM * h:ATT_V_DIM * (h + 1)].astype(F32)
            lane = lax.broadcasted_iota(jnp.int32, qh.shape, 1)
            qa_ref[h * grp:h * grp + tq, :] = jnp.where(lane < ATT_HEAD_DIM, qh, 0.0)
            qa_ref[h * grp + tq:(h + 1) * grp, :] = jnp.where(lane >= ATT_HEAD_DIM, qh, 0.0)
        m_ref[...] = jnp.full(m_ref.shape, NEG_INF, F32)
        l_ref[...] = jnp.zeros(l_ref.shape, F32)
        acc_ref[...] = jnp.zeros(acc_ref.shape, F32)

    def step(pages):
        qa = qa_ref[...].astype(BF16)
        tiles = []
        for k3, _, tab in pages:
            k2 = k3.reshape(-1, ATT_V_DIM).astype(BF16)
            s = lax.dot_general(qa, k2, NT, preferred_element_type=F32) + tab
            tiles += [s[:, c * LANES:(c + 1) * LANES] for c in range(-(-s.shape[1] // LANES))]
        m_prev = m_ref[...]
        row_max = jnp.max(functools.reduce(jnp.maximum, tiles), axis=1, keepdims=True)
        m_new = jnp.maximum(m_prev, row_max)
        alpha = jnp.exp2(m_prev - m_new)
        ps = [jnp.exp2(t - m_new[:, 0:t.shape[1]]) for t in tiles]
        row_sum = jnp.sum(functools.reduce(jnp.add, ps), axis=1, keepdims=True)
        l_ref[...] = alpha * l_ref[...] + row_sum
        acc = alpha * acc_ref[...]
        per_page = len(ps) // len(pages)
        for n, (_, v3, _) in enumerate(pages):
            v2 = v3.reshape(-1, ATT_V_DIM).astype(BF16)
            pb = jnp.concatenate([x.astype(BF16) for x in ps[n * per_page:(n + 1) * per_page]],
                                 axis=1)
            acc = acc + _bdot(pb, v2)
        acc_ref[...] = acc
        m_ref[...] = m_new

    def cached(last_kind):
        kinds = [0] * (DECODE_PAGES - 1) + [last_kind]
        return [(ck[...], cv[...], tab_ref[kind]) for ck, cv, kind in zip(ck_refs, cv_refs, kinds)]

    @pl.when(p < n_groups - 1)
    def _():
        step(cached(0))

    @pl.when(p == n_groups - 1)
    def _():
        step(cached(1))

    @pl.when(p == n_groups)
    def _():
        step([(kn_ref[0], vn_ref[0], tabn_ref[...])])
        o12 = acc_ref[...] / l_ref[...]
        for h in range(ATT_HEADS):
            o = o12[h * grp:h * grp + tq, :] - lam_ref[0] * o12[h * grp + tq:(h + 1) * grp, :]
            y = _rms(o, g_ref[...], SUBLN_EPS) * (1.0 - lam_init)
            o_ref[0, :, ATT_V_DIM * h:ATT_V_DIM * (h + 1)] = y


def _decode_bias_tables(rel_bias, tq, t_valid):
    tt = np.arange(tq)[:, None]
    cc = np.arange(PAGE_SIZE)[None, :]
    idx_far = np.full((tq, PAGE_SIZE), N_BUCKETS - 1)
    idx_last = _bucket_np(PAGE_SIZE - cc + tt)
    assert (_bucket_np(PAGE_SIZE + 1 + 0 * cc + tt) == N_BUCKETS - 1).all()
    idx_new = np.where((cc <= tt) & (cc < t_valid), _bucket_np(np.maximum(tt - cc, 0)), -1)
    tabs = _bias_tables(rel_bias, np.concatenate([idx_far, idx_last, idx_new], axis=0))
    h = ATT_HEADS
    tb = jnp.transpose(tabs.reshape(h, 3, tq, PAGE_SIZE), (1, 0, 2, 3))
    tb = jnp.broadcast_to(tb[:, :, None, :, :, None], (3, h, 2, tq, PAGE_SIZE, h))
    own = np.eye(h, dtype=bool)[None, :, None, None, None, :]
    tb = jnp.where(own, tb, NEG_INF)
    cached = tb[0:2].reshape(2, h * 2 * tq, PAGE_SIZE * h)
    new = tb[2, :, :, :, 0:tq, :].reshape(h * 2 * tq, tq * h)
    return cached, new


def _decode_attn(qb, k, v, cache_k, cache_v, layer, page_table, lam, tabs, subln_g, lam_init):
    bs, tq, d = qb.shape
    n_pages = page_table.shape[1]
    heads = (ATT_HEADS, ATT_V_DIM)
    kn = k.reshape(bs, tq, *heads)
    vn = v.reshape(bs, tq, *heads)
    tabs, tabs_new = tabs
    rows = ATT_HEADS * 2 * tq
    assert n_pages % DECODE_PAGES == 0
    n_groups = n_pages // DECODE_PAGES

    def page_map(which):
        def index(bi, p, pt):
            return (layer, pt[bi, DECODE_PAGES * jnp.minimum(p, n_groups - 1) + which], 0, 0, 0)
        return index

    seq3 = lambda bi, p, pt: (bi, 0, 0)
    seq4 = lambda bi, p, pt: (bi, 0, 0, 0)
    page = lambda which: pl.BlockSpec((None, None, PAGE_SIZE) + heads, page_map(which))
    new = pl.BlockSpec((1, tq) + heads, seq4)
    grid_spec = pltpu.PrefetchScalarGridSpec(
        num_scalar_prefetch=1,
        grid=(bs, n_groups + 1),
        in_specs=[pl.BlockSpec(memory_space=pltpu.SMEM),
                  pl.BlockSpec((1, tq, d), seq3),
                  *[page(n) for n in range(DECODE_PAGES)] * 2, new, new,
                  pl.BlockSpec(tabs.shape, lambda bi, p, pt: (0, 0, 0)),
                  pl.BlockSpec(tabs_new.shape, lambda bi, p, pt: (0, 0)),
                  pl.BlockSpec((1, ATT_V_DIM), lambda bi, p, pt: (0, 0))],
        out_specs=pl.BlockSpec((1, tq, d), seq3),
        scratch_shapes=[pltpu.VMEM((rows, ATT_V_DIM), F32),
                        pltpu.VMEM((rows, LANES), F32),
                        pltpu.VMEM((rows, LANES), F32),
                        pltpu.VMEM((rows, ATT_V_DIM), F32)])
    return pl.pallas_call(
        functools.partial(_decode_kernel, tq=tq, lam_init=lam_init),
        grid_spec=grid_spec,
        out_shape=jax.ShapeDtypeStruct((bs, tq, d), F32),
        compiler_params=_params(2),
        name="decode_diff_attn",
    )(page_table, lam, qb, *[cache_k] * DECODE_PAGES, *[cache_v] * DECODE_PAGES, kn, vn, tabs,
      tabs_new, subln_g.reshape(1, ATT_V_DIM))


def _rwkv_pre_kernel(*refs, has_vres, t_last):
    if has_vres:
        (x_ref, sh_ref, gm_ref, mu_ref, wrkv_ref, w0_ref, w1_ref, w2_ref, a0_ref, a1_ref, a2_ref,
         g1_ref, g2_ref, vf_ref, v0_ref, v1_ref, v2_ref,
         r_ref, k_ref, v_ref, lw_ref, a_ref, g_ref, last_ref, carry_ref) = refs
    else:
        (x_ref, sh_ref, gm_ref, mu_ref, wrkv_ref, w0_ref, w1_ref, w2_ref, a0_ref, a1_ref, a2_ref,
         g1_ref, g2_ref,
         r_ref, k_ref, v_ref, lw_ref, a_ref, g_ref, last_ref, carry_ref) = refs
    ti = pl.program_id(1)
    tm = x_ref.shape[1]
    xn = _rms(x_ref[0], gm_ref[...], NORM_EPS)

    @pl.when(ti == 0)
    def _():
        carry_ref[...] = sh_ref[0]

    rolled = pltpu.roll(xn, 1, axis=0)
    rid = lax.broadcasted_iota(jnp.int32, xn.shape, 0)
    x_prev = jnp.where(rid == 0, carry_ref[...], rolled)
    carry_ref[...] = xn[tm - 1:tm, :]

    @pl.when(ti == pl.num_programs(1) - 1)
    def _():
        last_ref[0] = xn[t_last:t_last + 1, :]

    xx = x_prev - xn
    mix = lambda n: (xn + xx * mu_ref[n:n + 1, :]).astype(BF16)
    xr, xw, xk, xv, xa, xg = [mix(n) for n in range(6)]
    r_ref[0] = _bdot(xr, wrkv_ref[0])
    k_ref[0] = _bdot(xk, wrkv_ref[1])
    v = _bdot(xv, wrkv_ref[2])
    if has_vres:
        lo = _bdot(_bdot(xv, v1_ref[...]).astype(BF16), v2_ref[...])
        v = v + (vf_ref[0] - v) * _sigmoid(v0_ref[...] + lo)
    v_ref[0] = v
    wl = w0_ref[...] + _bdot(jnp.tanh(_bdot(xw, w1_ref[...])).astype(BF16), w2_ref[...])
    z = -wl
    softplus = jnp.maximum(z, 0.0) + jnp.log(1.0 + jnp.exp(-jnp.abs(z)))
    lw_ref[0] = -jnp.exp(-softplus - 0.5)
    a_ref[0] = _sigmoid(a0_ref[...] + _bdot(_bdot(xa, a1_ref[...]).astype(BF16), a2_ref[...]))
    g_ref[0] = _bdot(_sigmoid(_bdot(xg, g1_ref[...])).astype(BF16), g2_ref[...])


def _pad_lora(w_in, w_out):
    n = w_in.shape[1]
    n_pad = -(-n // LANES) * LANES - n
    return (jnp.pad(w_in, ((0, 0), (0, n_pad))).astype(BF16),
            jnp.pad(w_out, ((0, n_pad), (0, 0))).astype(BF16))


def _rwkv_pre(x3, shift, g_mix, mu, wrkv_bf, w0, w1, w2, a0, a1, a2, g1, g2, vres, v_first, t_valid):
    nseq, t, d = x3.shape
    tm = min(RWKV_TILE, t)
    nt = t // tm
    t_last = (t_valid - 1) - (nt - 1) * tm
    tile = pl.BlockSpec((1, tm, d), lambda s, i: (s, i, 0))
    vec = _full((1, d))
    w1p, w2p = _pad_lora(w1, w2)
    a1p, a2p = _pad_lora(a1, a2)
    g1p, g2p = _pad_lora(g1, g2)
    ins = [x3, shift.reshape(nseq, 1, d), g_mix.reshape(1, d), mu, wrkv_bf,
           w0.reshape(1, d), w1p, w2p, a0.reshape(1, d), a1p, a2p, g1p, g2p]
    specs = [tile, pl.BlockSpec((1, 1, d), lambda s, i: (s, 0, 0)), vec, _full(mu.shape),
             pl.BlockSpec(wrkv_bf.shape, lambda s, i: (0, 0, 0), pipeline_mode=pl.Buffered(1)),
             vec, _full(w1p.shape), _full(w2p.shape),
             vec, _full(a1p.shape), _full(a2p.shape), _full(g1p.shape), _full(g2p.shape)]
    if vres is not None:
        v0, v1, v2 = vres
        v1p, v2p = _pad_lora(v1, v2)
        ins += [v_first, v0.reshape(1, d), v1p, v2p]
        specs += [tile, vec, _full(v1p.shape), _full(v2p.shape)]
    big = jax.ShapeDtypeStruct((nseq, t, d), F32)
    outs = pl.pallas_call(
        functools.partial(_rwkv_pre_kernel, has_vres=vres is not None, t_last=t_last),
        grid=(nseq, nt),
        in_specs=specs,
        out_specs=[tile] * 6 + [pl.BlockSpec((1, 1, d), lambda s, i: (s, 0, 0))],
        out_shape=[big] * 6 + [jax.ShapeDtypeStruct((nseq, 1, d), F32)],
        scratch_shapes=[pltpu.VMEM((1, d), F32)],
        compiler_params=_params(2),
        name="rwkv_pre",
    )(*ins)
    return outs


def _lane_half(shape, period=None):
    period = period or shape[1]
    lane = lax.broadcasted_iota(jnp.int32, shape, 1)
    return (lane % period) < period // 2


def _bd_rows(x, period=None):
    first = _lane_half(x.shape, period)
    zero = jnp.zeros_like(x)
    return jnp.concatenate([jnp.where(first, x, zero), jnp.where(first, zero, x)], axis=0)


def _bd_split(x, period=None):
    return tuple(_bd_rows(part, period) for part in _split(x))


def _pair_diag(full):
    n = full.shape[0] // 2
    return jnp.where(_lane_half((n, 2 * n)), full[0:n, :], full[n:2 * n, :])


def _pair_sum(x):
    first = _lane_half(x.shape)
    lo = jnp.sum(jnp.where(first, x, 0.0), axis=-1, keepdims=True)
    hi = jnp.sum(jnp.where(first, 0.0, x), axis=-1, keepdims=True)
    return jnp.where(first, lo, hi)


def _wkv_prep_kernel(r_ref, k_ref, v_ref, lw_ref, a_ref, kk_ref, ka_ref, rk_ref,
                     rp_ref, yc_ref, bn_ref, g_ref, h_ref, *, t_valid):
    ln = r_ref.shape[1]
    d = r_ref.shape[2]
    n = RW_HEAD
    pairs = range(d // (2 * n))
    sl = [slice(p * 2 * n, (p + 1) * 2 * n) for p in pairs]
    ri = lax.broadcasted_iota(jnp.int32, (ln, ln), 0)
    ci = lax.broadcasted_iota(jnp.int32, (ln, ln), 1)
    incl = (ri >= ci).astype(F32)
    incl2 = jnp.concatenate([incl, incl], axis=1)
    strict2 = jnp.concatenate([(ri > ci).astype(F32)] * 2, axis=1)
    eye_l2 = jnp.concatenate([(ri == ci).astype(F32)] * 2, axis=1)
    eye_n = (lax.broadcasted_iota(jnp.int32, (n, n), 0)
             == lax.broadcasted_iota(jnp.int32, (n, n), 1)).astype(F32)
    r, k, v, lw, a = r_ref[0], k_ref[0], v_ref[0], lw_ref[0], a_ref[0]
    if t_valid is not None:
        t_glob = pl.program_id(1) * ln + lax.broadcasted_iota(jnp.int32, (ln, 1), 0)
        live = (t_glob < t_valid).astype(F32)
        r, k, v, lw, a = r * live, k * live, v * live, lw * live, a * live

    cum = jnp.dot(incl, lw, precision=HIGHEST, preferred_element_type=F32)
    cum_last = cum[ln - 1:ln, :]
    e_neg = jnp.exp(-cum)
    e_out = jnp.exp(cum_last - cum)
    w_last = jnp.exp(cum_last)
    rt = r * jnp.exp(cum)
    kmod = k * (1.0 + (a - 1.0) * ka_ref[...])
    kt = kmod * e_neg
    ko = kmod * e_out
    kk0 = k * kk_ref[...]
    rkr = r * kmod * rk_ref[...]
    kk = jnp.concatenate(
        [kk0[:, sl[p]] / jnp.maximum(jnp.sqrt(_pair_sum(kk0[:, sl[p]] * kk0[:, sl[p]])), 1e-12)
         for p in pairs], axis=1)
    bn_ref[0] = jnp.concatenate([_pair_sum(rkr[:, sl[p]]) for p in pairs], axis=1) * v
    b_ = kk * a
    at_all = -kk * jnp.exp(cum - lw)
    bt_all = b_ * e_neg
    bo_all = b_ * e_out

    at = [at_all[:, sl[p]] for p in pairs]
    bo = [bo_all[:, sl[p]] for p in pairs]
    vp = [v[:, sl[p]] for p in pairs]
    cat0 = lambda u, w: tuple(jnp.concatenate([x, y], axis=0) for x, y in zip(u, w))
    cat1 = lambda u, w: tuple(jnp.concatenate([x, y], axis=1) for x, y in zip(u, w))
    vbd = [_bd_split(vp[p]) for p in pairs]
    sc = [_dot3p(_split(jnp.concatenate([at[p], rt[:, sl[p]]], axis=0)),
                 cat0(_bd_split(bt_all[:, sl[p]]), _bd_split(kt[:, sl[p]])), NT)
          for p in pairs]
    ab = [sc[p][0:ln, 0:2 * ln] * strict2 for p in pairs]
    ak = [sc[p][0:ln, 2 * ln:4 * ln] * strict2 for p in pairs]
    rb = [sc[p][ln:2 * ln, 0:2 * ln] * incl2 for p in pairs]
    rk = [sc[p][ln:2 * ln, 2 * ln:4 * ln] * incl2 for p in pairs]
    tinv = [eye_l2 + ab[p] for p in pairs]
    pw = ab
    pw_s = [_split(pw[p]) for p in pairs]
    for _ in range(int(math.log2(ln)) - 1):
        pw = [_dot3p(pw_s[p], tuple(_bd_rows(x) for x in pw_s[p])) for p in pairs]
        pw_s = [_split(pw[p]) for p in pairs]
        tinv = [tinv[p] + _dot3p(pw_s[p], _bd_split(tinv[p])) for p in pairs]
    akv = [_dot3p(_split(ak[p]), vbd[p]) for p in pairs]
    x = [_dot3p(_split(tinv[p]), cat1(_bd_split(at[p]), _bd_split(akv[p])))
         for p in pairs]
    zero = (jnp.zeros((2 * ln, 2 * n), BF16),) * 2
    y = [_dot3p(_split(jnp.concatenate([rb[p], rk[p]], axis=1)),
                cat0(_bd_split(x[p], 2 * n), cat1(zero, vbd[p])))
         for p in pairs]
    rp_ref[0] = rt + jnp.concatenate([y[p][:, 0:2 * n] for p in pairs], axis=1)
    yc_ref[0] = jnp.concatenate([y[p][:, 2 * n:4 * n] for p in pairs], axis=1)
    gm = [_pair_diag(_dot3(bo[p], x[p][:, 0:2 * n], TN)) for p in pairs]
    hm = [_pair_diag(_dot3(jnp.concatenate([bo[p], ko[:, sl[p]]], axis=0),
                           jnp.concatenate([x[p][:, 2 * n:4 * n], vp[p]], axis=0), TN))
          for p in pairs]
    g_ref[0, 0] = jnp.concatenate([eye_n] * (d // n), axis=1) * w_last + jnp.concatenate(gm, axis=1)
    h_ref[0, 0] = jnp.concatenate(hm, axis=1)


def _wkv_scan_kernel(rp_ref, yc_ref, bn_ref, g_ref, h_ref, s0_ref, lnw_ref, lnb_ref,
                     z_ref, sf_ref, s_ref):
    n = RW_HEAD
    d = rp_ref.shape[2]
    pairs = range(d // (2 * n))
    sl = [slice(p * 2 * n, (p + 1) * 2 * n) for p in pairs]

    @pl.when(pl.program_id(1) == 0)
    def _():
        s_ref[...] = s0_ref[0]

    s0 = [_bd_split(s_ref[:, sl[p]]) for p in pairs]
    y = [_dot3p(_split(rp_ref[0, :, sl[p]]), s0[p]) + yc_ref[0, :, sl[p]] for p in pairs]
    s1 = [_dot3p(_split(g_ref[0, 0, :, sl[p]]), s0[p]) + h_ref[0, 0, :, sl[p]] for p in pairs]
    s_new = jnp.concatenate(s1, axis=1)
    s_ref[...] = s_new
    sf_ref[0] = s_new
    yc = [y[p] - _pair_sum(y[p]) * (1.0 / n) for p in pairs]
    var = [_pair_sum(yc[p] * yc[p]) * (1.0 / n) for p in pairs]
    yn = jnp.concatenate([yc[p] * lax.rsqrt(var[p] + GN_EPS) for p in pairs], axis=1)
    z_ref[0] = yn * lnw_ref[...] + lnb_ref[...] + bn_ref[0]


def _wkv(r, k, v, lw, a, s0, k_k, k_a, r_k, ln_w, ln_b, t_valid):
    nseq, t, d = r.shape
    n = RW_HEAD
    ln = min(WKV_CHUNK, t)
    nc = t // ln
    tile = pl.BlockSpec((1, ln, d), lambda s, c: (s, c, 0))
    vec = _full((1, d))
    mat = pl.BlockSpec((1, 1, n, d), lambda s, c: (s, c, 0, 0))
    big = jax.ShapeDtypeStruct((nseq, t, d), F32)
    mats = jax.ShapeDtypeStruct((nseq, nc, n, d), F32)
    rp, yc, bn, g, hm = pl.pallas_call(
        functools.partial(_wkv_prep_kernel, t_valid=None if t_valid == t else t_valid),
        grid=(nseq, nc),
        in_specs=[tile] * 5 + [vec] * 3,
        out_specs=[tile] * 3 + [mat] * 2,
        out_shape=[big] * 3 + [mats] * 2,
        compiler_params=_params(2),
        name="wkv_prep",
    )(r, k, v, lw, a, k_k.reshape(1, d), k_a.reshape(1, d), r_k.reshape(1, d))
    s0_l = jnp.transpose(s0, (0, 3, 1, 2)).reshape(nseq, n, d)
    state = pl.BlockSpec((1, n, d), lambda s, c: (s, 0, 0))
    z, sf = pl.pallas_call(
        _wkv_scan_kernel,
        grid=(nseq, nc),
        in_specs=[tile] * 3 + [mat] * 2 + [state, vec, vec],
        out_specs=[tile, state],
        out_shape=[big, jax.ShapeDtypeStruct((nseq, n, d), F32)],
        scratch_shapes=[pltpu.VMEM((n, d), F32)],
        compiler_params=_params(2),
        name="wkv_scan",
    )(rp, yc, bn, g, hm, s0_l, ln_w.reshape(1, d), ln_b.reshape(1, d))
    return z, jnp.transpose(sf.reshape(nseq, n, d // n, n), (0, 2, 3, 1))


def kernel(x_prompt, x_sample, p_prompt, p_sample, cache_k, cache_v, state_wkv, state_shift, page_table, rel_bias, norm_mix, norm_mlp, norm_ple, norm_final, att_w_qkv, att_w_o, att_lam_q1, att_lam_k1, att_lam_q2, att_lam_k2, att_subln, rw_mu, rw_w_rkv, rw_w0, rw_w1, rw_w2, rw_a0, rw_a1, rw_a2, rw_v0, rw_v1, rw_v2, rw_g1, rw_g2, rw_k_k, rw_k_a, rw_r_k, rw_ln_w, rw_ln_b, rw_w_o, mlp_w1, mlp_w2, ple_w_in, ple_w_gate):
    depth = norm_mix.shape[0]
    bp, tp, d = x_prompt.shape
    bs, ts, _ = x_sample.shape
    ts_pad = -(-ts // SUBLANES) * SUBLANES
    heads_rw = d // RW_HEAD
    page_table = page_table.astype(jnp.int32)

    xp = x_prompt.reshape(bp * tp, d)
    xs = jnp.pad(x_sample, ((0, 0), (0, ts_pad - ts), (0, 0))).reshape(bs * ts_pad, d)
    pp_all = p_prompt.reshape(depth, bp * tp, -1)
    ps_all = jnp.pad(p_sample, ((0, 0), (0, 0), (0, ts_pad - ts), (0, 0))).reshape(depth, bs * ts_pad, -1)
    flash_tabs = _flash_bias_tables(rel_bias, tp)
    decode_tabs = _decode_bias_tables(rel_bias, ts_pad, ts)

    n_att = (depth + 1) // 2
    kv_prompt = None
    ks_l, vs_l = [], []
    wkvp_l, shp_l, wkvs_l, shs_l = [], [], [], []
    vf_p = vf_s = None
    y_p = y_s = None
    for i in range(depth):
        if i % 2 == 0:
            a = i // 2
            lam_init = 0.8 - 0.6 * math.exp(-0.3 * i)
            lam = _diff_lambda(att_lam_q1[a], att_lam_k1[a], att_lam_q2[a], att_lam_k2[a], lam_init)
            wqkv = att_w_qkv[a].astype(BF16)
            wo = att_w_o[a].astype(BF16)
            qb, k_all, v_all, kb, vb = _qkv_proj(xp, norm_mix[i], wqkv, a, n_att, kv_prompt)
            kv_prompt = (k_all, v_all)
            o = _flash_attn(qb.reshape(bp, tp, d), kb.reshape(bp, tp, d), vb.reshape(bp, tp, d),
                            lam, flash_tabs, att_subln[a], lam_init)
            xp = _proj_res(o.reshape(bp * tp, d), xp, wo)

            qb, k, v, _, _ = _qkv_proj(xs, norm_mix[i], wqkv)
            k = k.reshape(bs, ts_pad, d)
            v = v.reshape(bs, ts_pad, d)
            o = _decode_attn(qb.reshape(bs, ts_pad, d), k, v, cache_k, cache_v, a, page_table, lam,
                             decode_tabs, att_subln[a], lam_init)
            xs = _proj_res(o.reshape(bs * ts_pad, d), xs, wo)
            ks_l.append(k.reshape(bs, ts_pad, ATT_HEADS, ATT_V_DIM)[:, :ts])
            vs_l.append(v.reshape(bs, ts_pad, ATT_HEADS, ATT_V_DIM)[:, :ts])
        else:
            r = i // 2
            vres = None if r == 0 else (rw_v0[r - 1], rw_v1[r - 1], rw_v2[r - 1])
            wrkv = rw_w_rkv[r].astype(BF16)
            wo = rw_w_o[r].astype(BF16)
            lora = (rw_w0[r], rw_w1[r], rw_w2[r], rw_a0[r], rw_a1[r], rw_a2[r], rw_g1[r], rw_g2[r])
            heads = (rw_k_k[r], rw_k_a[r], rw_r_k[r].reshape(d), rw_ln_w[r], rw_ln_b[r])

            def time_mix(x2, nseq, t, shift, s0, v_first, t_valid):
                rr, kk, vv, lw, aa, gg, last = _rwkv_pre(
                    x2.reshape(nseq, t, d), shift, norm_mix[i], rw_mu[r], wrkv, *lora, vres,
                    v_first, t_valid)
                z, s_fin = _wkv(rr, kk, vv, lw, aa, s0, *heads, t_valid)
                out = _proj_res(z.reshape(nseq * t, d), x2, wo, gate2=gg.reshape(nseq * t, d))
                return out, s_fin, last.reshape(nseq, d), (vv if vres is None else v_first)

            xp, s_p, last_p, vf_p = time_mix(
                xp, bp, tp, jnp.zeros((bp, d), F32),
                jnp.zeros((bp, heads_rw, RW_HEAD, RW_HEAD), F32), vf_p, tp)
            xs, s_s, last_s, vf_s = time_mix(xs, bs, ts_pad, state_shift[r], state_wkv[r], vf_s, ts)
            wkvp_l.append(s_p)
            shp_l.append(last_p)
            wkvs_l.append(s_s)
            shs_l.append(last_s)
        final = i == depth - 1
        ffn_w = (norm_mlp[i], mlp_w1[i].astype(BF16), mlp_w2[i].astype(BF16), norm_ple[i],
                 ple_w_in[i].astype(BF16), ple_w_gate[i].astype(BF16), norm_final)
        xp, y_p = _ffn_ple(xp, pp_all[i], *ffn_w, final)
        xs, y_s = _ffn_ple(xs, ps_all[i], *ffn_w, final)

    y_prompt = y_p.reshape(bp, tp, d)
    y_sample = y_s.reshape(bs, ts_pad, d)[:, :ts]
    k_prompt, v_prompt = [z.reshape(n_att, bp, tp, ATT_HEADS, ATT_V_DIM) for z in kv_prompt]
    return (y_prompt, y_sample, k_prompt, v_prompt, jnp.stack(wkvp_l),
            jnp.stack(shp_l), jnp.stack(ks_l), jnp.stack(vs_l), jnp.stack(wkvs_l),
            jnp.stack(shs_l))
```
